```python
import jax, jax.numpy as jnp
from jax import lax
import numpy as np

D_MODEL = 1024
BATCH = 4
SEQ = 8192
DEPTH = 1

CHUNK = 64
EPS = 1e-6

LRU_WIDTH = D_MODEL
LRU_HEADS = 8
LRU_HEAD_DIM = LRU_WIDTH // LRU_HEADS
LRU_CONV = 4
LRU_C = 8.0

CONV_WIDTH = D_MODEL
CONV_GROUPS = 8
CONV_K = 3

N_BRANCHES = 2
IN_COLS = 2 * LRU_WIDTH + 3 * CONV_WIDTH + N_BRANCHES * D_MODEL

PEER_HEADS = 8
PEER_NKEYS = 128
PEER_N = PEER_NKEYS * PEER_NKEYS
PEER_DKEY = 256
PEER_HALF = PEER_DKEY // 2
PEER_TOPK = 16
PEER_BLOCK = 128

kernel_name = "hybrid_rglru_shortconv_peer"


def rmsnorm(x, g):
    xf = x.astype(jnp.float32)
    y = xf * lax.rsqrt(jnp.mean(xf * xf, axis=-1, keepdims=True) + EPS)
    return (y * g.astype(jnp.float32)).astype(x.dtype)


def causal_dwconv(x, w):
    k, c = w.shape
    return lax.conv_general_dilated(
        x, w[:, None, :].astype(x.dtype), window_strides=(1,), padding=[(k - 1, 0)],
        dimension_numbers=("NWC", "WIO", "NWC"), feature_group_count=c)


def rg_lru(x, w_a, b_a, w_x, b_x, lam):
    b, s, _ = x.shape
    xh = x.reshape(b, s, LRU_HEADS, LRU_HEAD_DIM)
    r = jax.nn.sigmoid(jnp.einsum("bshi,hij->bshj", xh, w_a).reshape(b, s, LRU_WIDTH) + b_a)
    i = jax.nn.sigmoid(jnp.einsum("bshi,hij->bshj", xh, w_x).reshape(b, s, LRU_WIDTH) + b_x)
    log_a = (-LRU_C * r.astype(jnp.float32)) * jax.nn.softplus(-lam.astype(jnp.float32))
    a = jnp.exp(log_a)
    mult = jnp.sqrt(-jnp.expm1(2.0 * log_a))
    u = mult * (i * x).astype(jnp.float32)

    def combine(left, right):
        a1, b1 = left
        a2, b2 = right
        return a1 * a2, a2 * b1 + b2

    _, h = lax.associative_scan(combine, (a, u), axis=1)
    return h.astype(x.dtype)


def token_mixers(xn, w_in, conv_a_w, conv_a_b, w_a, b_a, w_x, b_x, lam, conv_b_w, w_out):
    p = xn @ w_in
    splits = [LRU_WIDTH, 2 * LRU_WIDTH, 2 * LRU_WIDTH + CONV_WIDTH,
              2 * LRU_WIDTH + 2 * CONV_WIDTH, 2 * LRU_WIDTH + 3 * CONV_WIDTH,
              2 * LRU_WIDTH + 3 * CONV_WIDTH + D_MODEL]
    xa, ga, vb, bb, cb, ma, mb = jnp.split(p, splits, axis=-1)
    ya = rg_lru(causal_dwconv(xa, conv_a_w) + conv_a_b, w_a, b_a, w_x, b_x, lam) * jax.nn.gelu(ga)
    yb = bb * causal_dwconv(cb * vb, conv_b_w)
    merged = jax.nn.sigmoid(ma) * ya + jax.nn.sigmoid(mb) * yb
    return merged @ w_out


def peer(xn, w_q, sub_keys, u_tab, v_tab):
    b, s, d = xn.shape
    t = xn.reshape(-1, d)
    n_tok = t.shape[0]
    q = (t @ w_q).reshape(n_tok, PEER_HEADS, 2, PEER_HALF)
    sc = jnp.einsum("thpk,hpnk->thpn", q, sub_keys).astype(jnp.float32)
    s_top, i_top = lax.top_k(sc, PEER_TOPK)
    cand = s_top[:, :, 0, :, None] + s_top[:, :, 1, None, :]
    cand_idx = i_top[:, :, 0, :, None] * PEER_NKEYS + i_top[:, :, 1, None, :]
    kk = PEER_TOPK * PEER_TOPK
    best, pos = lax.top_k(cand.reshape(n_tok, PEER_HEADS, kk), PEER_TOPK)
    idx = jnp.take_along_axis(cand_idx.reshape(n_tok, PEER_HEADS, kk), pos, axis=-1)
    g = jax.nn.softmax(best, axis=-1).astype(xn.dtype)

    nb = n_tok // PEER_BLOCK

    def block(args):
        tb, ib, gb = args
        ue = u_tab[ib]
        act = jax.nn.gelu(jnp.einsum("phkd,pd->phk", ue, tb))
        ve = v_tab[ib]
        return jnp.einsum("phk,phkd->pd", gb * act, ve)

    out = lax.map(block, (t.reshape(nb, PEER_BLOCK, d),
                          idx.reshape(nb, PEER_BLOCK, PEER_HEADS, PEER_TOPK),
                          g.reshape(nb, PEER_BLOCK, PEER_HEADS, PEER_TOPK)))
    return out.reshape(b, s, d)


def setup_inputs(seed: int = 0) -> dict:
    key = jax.random.key(seed)
    ks = jax.random.split(key, 20)
    f = jnp.float32
    L = DEPTH

    def nrm(k, shape, scale):
        return jax.random.normal(k, shape, f) * scale

    x = nrm(ks[0], (BATCH, SEQ, D_MODEL), 1.0)
    norm1_g = 1.0 + nrm(ks[1], (L, D_MODEL), 0.02)
    w_in = nrm(ks[2], (L, D_MODEL, IN_COLS), D_MODEL ** -0.5)
    conv_a_w = nrm(ks[3], (L, LRU_CONV, LRU_WIDTH), LRU_CONV ** -0.5)
    conv_a_b = nrm(ks[4], (L, LRU_WIDTH), 0.01)
    w_a = nrm(ks[5], (L, LRU_HEADS, LRU_HEAD_DIM, LRU_HEAD_DIM), LRU_HEAD_DIM ** -0.5)
    b_a = nrm(ks[6], (L, LRU_WIDTH), 0.01)
    w_x = nrm(ks[7], (L, LRU_HEADS, LRU_HEAD_DIM, LRU_HEAD_DIM), LRU_HEAD_DIM ** -0.5)
    b_x = nrm(ks[8], (L, LRU_WIDTH), 0.01)
    a_init = jax.random.uniform(ks[9], (L, LRU_WIDTH), f, 0.9, 0.999)
    sig = a_init ** (1.0 / LRU_C)
    lru_lambda = jnp.log(sig) - jnp.log1p(-sig)
    conv_b_w = nrm(ks[10], (L, CONV_K, CONV_WIDTH), CONV_K ** -0.5)
    w_out = nrm(ks[11], (L, D_MODEL, D_MODEL), D_MODEL ** -0.5)
    norm2_g = 1.0 + nrm(ks[12], (L, D_MODEL), 0.02)
    peer_wq = nrm(ks[13], (L, D_MODEL, PEER_HEADS * PEER_DKEY), D_MODEL ** -0.5)
    peer_subkeys = nrm(ks[14], (L, PEER_HEADS, 2, PEER_NKEYS, PEER_HALF), PEER_HALF ** -0.5)
    peer_u = nrm(ks[15], (L, PEER_N, D_MODEL), D_MODEL ** -0.5)
    peer_v = nrm(ks[16], (L, PEER_N, D_MODEL), PEER_HEADS ** -0.5)
    final_g = 1.0 + nrm(ks[17], (D_MODEL,), 0.02)
    return {"x": x, "norm1_g": norm1_g, "w_in": w_in, "conv_a_w": conv_a_w, "conv_a_b": conv_a_b,
            "w_a": w_a, "b_a": b_a, "w_x": w_x, "b_x": b_x, "lru_lambda": lru_lambda,
            "conv_b_w": conv_b_w, "w_out": w_out, "norm2_g": norm2_g, "peer_wq": peer_wq,
            "peer_subkeys": peer_subkeys, "peer_u": peer_u, "peer_v": peer_v, "final_g": final_g}


def reference(x, norm1_g, w_in, conv_a_w, conv_a_b, w_a, b_a, w_x, b_x, lru_lambda,
              conv_b_w, w_out, norm2_g, peer_wq, peer_subkeys, peer_u, peer_v, final_g):
    h = x
    for l in range(DEPTH):
        xn = rmsnorm(h, norm1_g[l])
        h = h + token_mixers(xn, w_in[l], conv_a_w[l], conv_a_b[l], w_a[l], b_a[l], w_x[l], b_x[l],
                             lru_lambda[l], conv_b_w[l], w_out[l])
        h = h + peer(rmsnorm(h, norm2_g[l]), peer_wq[l], peer_subkeys[l], peer_u[l], peer_v[l])
    return rmsnorm(h, final_g)
```

```python
import functools
import math

import jax
import jax.numpy as jnp
from jax import lax
from jax.experimental import pallas as pl
from jax.experimental.pallas import tpu as pltpu

F32 = jnp.float32
BF16 = jnp.bfloat16
I32 = jnp.int32

EPS = 1e-6
LRU_C = 8.0
LANES = 128
SUBLANES = 8
TOPK = 16
NKEYS = 128
VMEM_LIMIT = 56 * 1024 * 1024

MIX_TS = 256
ROUTE_TT = 256
PEER_TB = 64


def _gelu(x):
    return 0.5 * x * (1.0 + jnp.tanh(math.sqrt(2.0 / math.pi) * (x + 0.044715 * (x * x * x))))


def _sigmoid(x):
    return 0.5 * (jnp.tanh(0.5 * x) + 1.0)


def _mm(a, b):
    return jnp.dot(a, b, preferred_element_type=F32)


def _mm_nt(a, b):
    return lax.dot_general(a, b, (((1,), (1,)), ((), ())), preferred_element_type=F32)


def _rms(x, g):
    return x * lax.rsqrt(jnp.mean(x * x, axis=-1, keepdims=True) + EPS) * g


def _mixer_kernel(x_ref, g1_ref, win_ref, caw_ref, cab_ref, wax_ref, ba_ref, bx_ref, lam_ref, cbw_ref,
                  wout_ref, o_ref, xa_ext, cv_ext, a_s, u_s, h_s, hc_s):
    ts, d = a_s.shape
    hd = wax_ref.shape[1]
    n_heads = wax_ref.shape[0]

    @pl.when(pl.program_id(1) == 0)
    def _():
        xa_ext[0:SUBLANES, :] = jnp.zeros((SUBLANES, d), F32)
        cv_ext[0:SUBLANES, :] = jnp.zeros((SUBLANES, d), F32)
        hc_s[...] = jnp.zeros_like(hc_s)

    x = x_ref[0]
    xn = _rms(x, g1_ref[...]).astype(BF16)

    def proj(j):
        return _mm(xn, win_ref[:, j * d:(j + 1) * d])

    xa_ext[SUBLANES:SUBLANES + ts, :] = proj(0)
    xc = cab_ref[...]
    for k in range(4):
        xc = xc + caw_ref[k:k + 1, :] * xa_ext[SUBLANES - 3 + k:SUBLANES - 3 + k + ts, :]
    xa_ext[0:SUBLANES, :] = xa_ext[ts:ts + SUBLANES, :]

    lam = lam_ref[...]
    neg_c_sp = -LRU_C * (jnp.maximum(-lam, 0.0) + jnp.log1p(jnp.exp(-jnp.abs(lam))))
    xcb = xc.astype(BF16)
    for h in range(n_heads):
        sl = slice(h * hd, (h + 1) * hd)
        gates = _mm(xcb[:, sl], wax_ref[h])
        r = _sigmoid(gates[:, :hd] + ba_ref[:, sl])
        i = _sigmoid(gates[:, hd:] + bx_ref[:, sl])
        log_a = r * neg_c_sp[:, sl]
        a = jnp.exp(log_a)
        mult = jnp.sqrt(1.0 - a * a)
        a_s[:, sl] = a
        u_s[:, sl] = mult * (i * xc[:, sl])

    row = lax.broadcasted_iota(I32, (SUBLANES, d), 0)

    def scan_body(g, hprev):
        r0 = pl.multiple_of(g * SUBLANES, SUBLANES)
        av = a_s[pl.ds(r0, SUBLANES), :]
        uv = u_s[pl.ds(r0, SUBLANES), :]
        for sh in (1, 2, 4):
            keep = row >= sh
            a_sh = jnp.where(keep, pltpu.roll(av, sh, axis=0), 1.0)
            u_sh = jnp.where(keep, pltpu.roll(uv, sh, axis=0), 0.0)
            uv = uv + av * u_sh
            av = av * a_sh
        hv = uv + av * hprev
        h_s[pl.ds(r0, SUBLANES), :] = hv
        return hv[SUBLANES - 1:SUBLANES, :]

    hc_s[...] = lax.fori_loop(0, ts // SUBLANES, scan_body, hc_s[...])

    ya = h_s[...] * _gelu(proj(1))

    cv_ext[SUBLANES:SUBLANES + ts, :] = proj(4) * proj(2)
    conv = cbw_ref[0:1, :] * cv_ext[SUBLANES - 2:SUBLANES - 2 + ts, :]
    for k in range(1, 3):
        conv = conv + cbw_ref[k:k + 1, :] * cv_ext[SUBLANES - 2 + k:SUBLANES - 2 + k + ts, :]
    cv_ext[0:SUBLANES, :] = cv_ext[ts:ts + SUBLANES, :]
    yb = proj(3) * conv

    merged = _sigmoid(proj(5)) * ya + _sigmoid(proj(6)) * yb
    o_ref[0] = x + _mm(merged.astype(BF16), wout_ref[...])


def _mixer(x, g1, w_in, caw, cab, wax, ba, bx, lam, cbw, w_out, ts):
    b, s, d = x.shape
    const = lambda shape: pl.BlockSpec(shape, lambda i, j: (0,) * len(shape))
    return pl.pallas_call(
        _mixer_kernel,
        grid=(b, s // ts),
        in_specs=[pl.BlockSpec((1, ts, d), lambda i, j: (i, j, 0)),
                  const(g1.shape), const(w_in.shape), const(caw.shape), const(cab.shape), const(wax.shape),
                  const(ba.shape), const(bx.shape), const(lam.shape), const(cbw.shape), const(w_out.shape)],
        out_specs=pl.BlockSpec((1, ts, d), lambda i, j: (i, j, 0)),
        out_shape=jax.ShapeDtypeStruct(x.shape, F32),
        scratch_shapes=[pltpu.VMEM((ts + SUBLANES, d), F32), pltpu.VMEM((ts + SUBLANES, d), F32),
                        pltpu.VMEM((ts, d), F32), pltpu.VMEM((ts, d), F32), pltpu.VMEM((ts, d), F32),
                        pltpu.VMEM((1, d), F32)],
        compiler_params=pltpu.CompilerParams(dimension_semantics=("arbitrary", "arbitrary"),
                                             vmem_limit_bytes=VMEM_LIMIT),
        name="mixer",
    )(x, g1, w_in, caw, cab, wax, ba, bx, lam, cbw, w_out)


def _extract_top(vals, keys, payload, n, key_sentinel):
    out_v, out_k, out_p = [], [], []
    neg_inf = jnp.float32(-jnp.inf)
    for _ in range(n):
        m = vals[0]
        for v in vals[1:]:
            m = jnp.maximum(m, v)
        m = jnp.max(m, axis=0, keepdims=True)
        cand = [jnp.where(v == m, k, key_sentinel) for v, k in zip(vals, keys)]
        am = cand[0]
        for c in cand[1:]:
            am = jnp.minimum(am, c)
        am = jnp.min(am, axis=0, keepdims=True)
        hit = [k == am for k in keys]
        out_v.append(m)
        out_k.append(am)
        if payload is not None:
            p = jnp.where(hit[0], payload[0], -1)
            for hh, pp in zip(hit[1:], payload[1:]):
                p = jnp.maximum(p, jnp.where(hh, pp, -1))
            out_p.append(jnp.max(p, axis=0, keepdims=True))
        vals = [jnp.where(hh, neg_inf, v) for hh, v in zip(hit, vals)]
    return out_v, out_k, out_p


def _route_kernel(h_ref, g2_ref, wq_ref, sk_ref, xn_ref, idx_ref, gate_ref, q_s, st_s, it_s, e_s, g_s):
    tt, d = h_ref.shape
    n_hp = sk_ref.shape[0]
    n_heads = n_hp // 2
    n_lt = tt // LANES

    xn = _rms(h_ref[...], g2_ref[...]).astype(BF16)
    xn_ref[...] = xn
    q = _mm(xn, wq_ref[...]).astype(BF16)
    for hp in range(n_hp):
        q_s[hp] = q[:, hp * LANES:(hp + 1) * LANES]

    sub = lax.broadcasted_iota(I32, (SUBLANES, LANES), 0)

    def stage1(hp, carry):
        sc = _mm_nt(sk_ref[hp], q_s[hp])
        for lt in range(n_lt):
            blk = [sc[j * SUBLANES:(j + 1) * SUBLANES, lt * LANES:(lt + 1) * LANES]
                   for j in range(NKEYS // SUBLANES)]
            keys = [sub + j * SUBLANES for j in range(NKEYS // SUBLANES)]
            v, k, _ = _extract_top(blk, keys, None, TOPK, NKEYS)
            st_s[hp, :, lt * LANES:(lt + 1) * LANES] = jnp.concatenate(v, axis=0)
            it_s[hp, :, lt * LANES:(lt + 1) * LANES] = jnp.concatenate(k, axis=0)
        return carry

    lax.fori_loop(0, n_hp, stage1, 0)

    big = TOPK * TOPK
    neg_inf = jnp.float32(-jnp.inf)

    def stage2(hh, carry):
        for lt in range(n_lt):
            ls = slice(lt * LANES, (lt + 1) * LANES)
            s0 = st_s[2 * hh, :, ls]
            s1 = st_s[2 * hh + 1, :, ls]
            i0 = it_s[2 * hh, :, ls] * NKEYS
            i1 = it_s[2 * hh + 1, :, ls]
            vals, keys, pay = [], [], []
            for i, jb, jmax in ((0, 0, 8), (0, 1, 8), (1, 0, 8), (2, 0, 5), (3, 0, 4)):
                v = s0[i:i + 1, :] + s1[jb * 8:(jb + 1) * 8, :]
                kk = sub + (i * TOPK + jb * 8)
                pp = i0[i:i + 1, :] + i1[jb * 8:(jb + 1) * 8, :]
                if jmax < 8:
                    ok = sub < jmax
                    v = jnp.where(ok, v, neg_inf)
                    kk = jnp.where(ok, kk, big)
                vals.append(v); keys.append(kk); pay.append(pp)
            for j, ib, lo, hi in ((0, 0, 4, 8), (0, 1, 0, 8), (1, 0, 4, 8), (2, 0, 4, 5)):
                v = s0[ib * 8:(ib + 1) * 8, :] + s1[j:j + 1, :]
                kk = (sub + ib * 8) * TOPK + j
                pp = i0[ib * 8:(ib + 1) * 8, :] + i1[j:j + 1, :]
                if lo > 0 or hi < 8:
                    ok = (sub >= lo) & (sub < hi)
                    v = jnp.where(ok, v, neg_inf)
                    kk = jnp.where(ok, kk, big)
                vals.append(v); keys.append(kk); pay.append(pp)
            bv, _, bp = _extract_top(vals, keys, pay, TOPK, big)
            best = jnp.concatenate(bv, axis=0)
            ex = jnp.exp(best - best[0:1, :])
            g_s[hh, :, ls] = ex / jnp.sum(ex, axis=0, keepdims=True)
            e_s[hh, :, ls] = jnp.concatenate(bp, axis=0)
        return carry

    lax.fori_loop(0, n_heads, stage2, 0)

    idx_ref[...] = (e_s[...].reshape(n_heads * TOPK, tt) * 4).T
    gate_ref[...] = g_s[...].reshape(n_heads * TOPK, tt).T


def _route(h1, g2, wq, sk, tt):
    t, d = h1.shape
    n_hp = sk.shape[0]
    n_heads = n_hp // 2
    const = lambda shape: pl.BlockSpec(shape, lambda i: (0,) * len(shape))
    return pl.pallas_call(
        _route_kernel,
        grid=(t // tt,),
        in_specs=[pl.BlockSpec((tt, d), lambda i: (i, 0)), const(g2.shape), const(wq.shape), const(sk.shape)],
        out_specs=[pl.BlockSpec((tt, d), lambda i: (i, 0)),
                   pl.BlockSpec((tt, n_heads * TOPK), lambda i: (i, 0)),
                   pl.BlockSpec((tt, n_heads * TOPK), lambda i: (i, 0))],
        out_shape=[jax.ShapeDtypeStruct((t, d), BF16),
                   jax.ShapeDtypeStruct((t, n_heads * TOPK), I32),
                   jax.ShapeDtypeStruct((t, n_heads * TOPK), F32)],
        scratch_shapes=[pltpu.VMEM((n_hp, tt, LANES), BF16),
                        pltpu.VMEM((n_hp, TOPK, tt), F32), pltpu.VMEM((n_hp, TOPK, tt), I32),
                        pltpu.VMEM((n_heads, TOPK, tt), I32), pltpu.VMEM((n_heads, TOPK, tt), F32)],
        compiler_params=pltpu.CompilerParams(dimension_semantics=("arbitrary",),
                                             vmem_limit_bytes=VMEM_LIMIT),
        name="route",
    )(h1, g2, wq, sk)


def _peer_u_kernel(idx_ref, x_ref, gate_ref, sum_ref, tab_ref, o_ref, p_s):
    tb, nsel = gate_ref.shape

    def body(t, carry):
        xw = x_ref[pl.ds(pl.multiple_of(t * 4, 4), 4), :]
        xb = pltpu.bitcast(xw, BF16)
        for k in range(nsel):
            row = tab_ref[pl.ds(pl.multiple_of(idx_ref[t, k], 4), 4), :]
            p_s[k * SUBLANES:(k + 1) * SUBLANES, :] = pltpu.bitcast(row, BF16) * xb
        r = _mm(sum_ref[...], p_s[...])
        act = jnp.sum(r.T, axis=0, keepdims=True)
        o_ref[pl.ds(t, 1), :] = gate_ref[pl.ds(t, 1), :] * _gelu(act)
        return carry

    lax.fori_loop(0, tb, body, 0)


def _peer_u(idx4, xpk, gate, summat, table, tb):
    t, nsel = gate.shape
    return pl.pallas_call(
        _peer_u_kernel,
        grid=(t // tb,),
        in_specs=[pl.BlockSpec((tb, nsel), lambda i: (i, 0), memory_space=pltpu.SMEM),
                  pl.BlockSpec((tb * 4, LANES), lambda i: (i, 0)),
                  pl.BlockSpec((tb, nsel), lambda i: (i, 0)),
                  pl.BlockSpec(memory_space=pltpu.VMEM),
                  pl.BlockSpec(memory_space=pltpu.VMEM)],
        out_specs=pl.BlockSpec((tb, nsel), lambda i: (i, 0)),
        out_shape=jax.ShapeDtypeStruct((t, nsel), F32),
        scratch_shapes=[pltpu.VMEM((nsel * SUBLANES, LANES), BF16)],
        compiler_params=pltpu.CompilerParams(dimension_semantics=("arbitrary",),
                                             vmem_limit_bytes=VMEM_LIMIT),
        name="peer_u",
    )(idx4, xpk, gate, summat, table)


def _peer_v_kernel(idx_ref, w_ref, tab_ref, o_ref):
    tb, nsel = w_ref.shape
    n_acc = 2

    def body(t, carry):
        acc_hi = [jnp.zeros((4, LANES), F32) for _ in range(n_acc)]
        acc_lo = [jnp.zeros((4, LANES), F32) for _ in range(n_acc)]
        for k in range(nsel):
            row = tab_ref[pl.ds(pl.multiple_of(idx_ref[t, k], 4), 4), :]
            w = w_ref[t, k]
            hi = pltpu.bitcast(row & jnp.int32(-65536), F32)
            lo = pltpu.bitcast(row << 16, F32)
            acc_hi[k % n_acc] = acc_hi[k % n_acc] + w * hi
            acc_lo[k % n_acc] = acc_lo[k % n_acc] + w * lo
        r0 = pl.multiple_of(t * SUBLANES, SUBLANES)
        o_ref[pl.ds(r0, 4), :] = acc_hi[0] + acc_hi[1]
        o_ref[pl.ds(r0 + 4, 4), :] = acc_lo[0] + acc_lo[1]
        return carry

    lax.fori_loop(0, tb, body, 0)


def _peer_v(idx4, w, table, tb):
    t, nsel = w.shape
    return pl.pallas_call(
        _peer_v_kernel,
        grid=(t // tb,),
        in_specs=[pl.BlockSpec((tb, nsel), lambda i: (i, 0), memory_space=pltpu.SMEM),
                  pl.BlockSpec((tb, nsel), lambda i: (i, 0), memory_space=pltpu.SMEM),
                  pl.BlockSpec(memory_space=pltpu.VMEM)],
        out_specs=pl.BlockSpec((tb * SUBLANES, LANES), lambda i: (i, 0)),
        out_shape=jax.ShapeDtypeStruct((t * SUBLANES, LANES), F32),
        compiler_params=pltpu.CompilerParams(dimension_semantics=("arbitrary",),
                                             vmem_limit_bytes=VMEM_LIMIT),
        name="peer_v",
    )(idx4, w, table)


def _final_kernel(h_ref, p_ref, g_ref, o_ref):
    o_ref[...] = _rms(h_ref[...] + p_ref[...], g_ref[...])


def _final(h1, pout, g, tt):
    t, d = h1.shape
    return pl.pallas_call(
        _final_kernel,
        grid=(t // tt,),
        in_specs=[pl.BlockSpec((tt, d), lambda i: (i, 0)), pl.BlockSpec((tt, d), lambda i: (i, 0)),
                  pl.BlockSpec((1, d), lambda i: (0, 0))],
        out_specs=pl.BlockSpec((tt, d), lambda i: (i, 0)),
        out_shape=jax.ShapeDtypeStruct((t, d), F32),
        compiler_params=pltpu.CompilerParams(dimension_semantics=("arbitrary",)),
        name="final",
    )(h1, pout, g)


def _bf16_bits(a):
    return lax.bitcast_convert_type(a.astype(BF16), jnp.uint16).astype(jnp.uint32)


def _pack_sublane_pairs(a):
    n, d = a.shape
    bits = _bf16_bits(a).reshape(n, d // (2 * LANES), 2, LANES)
    words = bits[:, :, 0, :] | (bits[:, :, 1, :] << 16)
    return lax.bitcast_convert_type(words, I32).reshape(n * d // (2 * LANES), LANES)


def _pack_halves(a):
    n, d = a.shape
    bits = _bf16_bits(a)
    words = (bits[:, :d // 2] << 16) | bits[:, d // 2:]
    return lax.bitcast_convert_type(words, I32).reshape(n * d // (2 * LANES), LANES)


def kernel(x, norm1_g, w_in, conv_a_w, conv_a_b, w_a, b_a, w_x, b_x, lru_lambda, conv_b_w, w_out, norm2_g,
           peer_wq, peer_subkeys, peer_u, peer_v, final_g):
    b, s, d = x.shape
    depth = norm1_g.shape[0]
    t = b * s
    h = x
    for l in range(depth):
        wax = jnp.concatenate([w_a[l], w_x[l]], axis=-1).astype(BF16)
        h = _mixer(h, norm1_g[l][None], w_in[l].astype(BF16), conv_a_w[l], conv_a_b[l][None], wax,
                   b_a[l][None], b_x[l][None], lru_lambda[l][None], conv_b_w[l], w_out[l].astype(BF16),
                   min(MIX_TS, s))
        h1 = h.reshape(t, d)
        n_heads, _, nkeys, half = peer_subkeys[l].shape
        sk = peer_subkeys[l].reshape(n_heads * 2, nkeys, half).astype(BF16)
        xn2, idx4, gate = _route(h1, norm2_g[l][None], peer_wq[l].astype(BF16), sk, min(ROUTE_TT, t))
        nsel = n_heads * TOPK
        summat = (jnp.arange(nsel * SUBLANES)[None, :] // SUBLANES == jnp.arange(nsel)[:, None]).astype(BF16)
        wts = _peer_u(idx4, _pack_sublane_pairs(xn2), gate, summat, _pack_sublane_pairs(peer_u[l]),
                      min(PEER_TB, t))
        pout = _peer_v(idx4, wts, _pack_halves(peer_v[l]), min(PEER_TB, t)).reshape(t, d)
        if l + 1 < depth:
            h = (h1 + pout).reshape(b, s, d)
    return _final(h1, pout, final_g[None], min(ROUTE_TT, t)).reshape(b, s, d)
```

```python
import math

import jax
import jax.numpy as jnp
from jax import lax
from jax.experimental import pallas as pl
from jax.experimental.pallas import tpu as pltpu

F32 = jnp.float32
BF16 = jnp.bfloat16
I32 = jnp.int32

EPS = 1e-6
LRU_C = 8.0
LANES = 128
SUBLANES = 8
TOPK = 16
NKEYS = 128
VMEM_LIMIT = 56 * 1024 * 1024

MIX_TS = 256
ROUTE_TT = 256
PEER_TB = 256
PEER_GROUP = 8


def _gelu(x):
    return 0.5 * x * (1.0 + jnp.tanh(math.sqrt(2.0 / math.pi) * (x + 0.044715 * (x * x * x))))


def _sigmoid(x):
    return 0.5 * (jnp.tanh(0.5 * x) + 1.0)


def _mm(a, b):
    return jnp.dot(a, b, preferred_element_type=F32)


def _mm_nt(a, b):
    return lax.dot_general(a, b, (((1,), (1,)), ((), ())), preferred_element_type=F32)


def _rms(x, g):
    return x * lax.rsqrt(jnp.mean(x * x, axis=-1, keepdims=True) + EPS) * g


def _mixer_kernel(x_ref, g1_ref, win_ref, caw_ref, cab_ref, wax_ref, ba_ref, bx_ref, lam_ref, cbw_ref,
                  wout_ref, o_ref, xa_ext, cv_ext, a_s, u_s, h_s, hc_s):
    ts, d = a_s.shape
    hd = wax_ref.shape[1]
    n_heads = wax_ref.shape[0]

    @pl.when(pl.program_id(1) == 0)
    def _():
        xa_ext[0:SUBLANES, :] = jnp.zeros((SUBLANES, d), F32)
        cv_ext[0:SUBLANES, :] = jnp.zeros((SUBLANES, d), F32)
        hc_s[...] = jnp.zeros_like(hc_s)

    x = x_ref[0]
    xn = _rms(x, g1_ref[...]).astype(BF16)

    def proj(j):
        return _mm(xn, win_ref[:, j * d:(j + 1) * d])

    xa_ext[SUBLANES:SUBLANES + ts, :] = proj(0)
    xc = cab_ref[...]
    for k in range(4):
        xc = xc + caw_ref[k:k + 1, :] * xa_ext[SUBLANES - 3 + k:SUBLANES - 3 + k + ts, :]
    xa_ext[0:SUBLANES, :] = xa_ext[ts:ts + SUBLANES, :]

    lam = lam_ref[...]
    neg_c_sp = -LRU_C * (jnp.maximum(-lam, 0.0) + jnp.log1p(jnp.exp(-jnp.abs(lam))))
    xcb = xc.astype(BF16)
    for h in range(n_heads):
        sl = slice(h * hd, (h + 1) * hd)
        gates = _mm(xcb[:, sl], wax_ref[h])
        r = _sigmoid(gates[:, :hd] + ba_ref[:, sl])
        i = _sigmoid(gates[:, hd:] + bx_ref[:, sl])
        log_a = r * neg_c_sp[:, sl]
        a = jnp.exp(log_a)
        mult = jnp.sqrt(1.0 - a * a)
        a_s[:, sl] = a
        u_s[:, sl] = mult * (i * xc[:, sl])

    row = lax.broadcasted_iota(I32, (SUBLANES, d), 0)

    def scan_body(g, hprev):
        r0 = pl.multiple_of(g * SUBLANES, SUBLANES)
        av = a_s[pl.ds(r0, SUBLANES), :]
        uv = u_s[pl.ds(r0, SUBLANES), :]
        for sh in (1, 2, 4):
            keep = row >= sh
            a_sh = jnp.where(keep, pltpu.roll(av, sh, axis=0), 1.0)
            u_sh = jnp.where(keep, pltpu.roll(uv, sh, axis=0), 0.0)
            uv = uv + av * u_sh
            av = av * a_sh
        hv = uv + av * hprev
        h_s[pl.ds(r0, SUBLANES), :] = hv
        return hv[SUBLANES - 1:SUBLANES, :]

    hc_s[...] = lax.fori_loop(0, ts // SUBLANES, scan_body, hc_s[...])

    ya = h_s[...] * _gelu(proj(1))

    cv_ext[SUBLANES:SUBLANES + ts, :] = proj(4) * proj(2)
    conv = cbw_ref[0:1, :] * cv_ext[SUBLANES - 2:SUBLANES - 2 + ts, :]
    for k in range(1, 3):
        conv = conv + cbw_ref[k:k + 1, :] * cv_ext[SUBLANES - 2 + k:SUBLANES - 2 + k + ts, :]
    cv_ext[0:SUBLANES, :] = cv_ext[ts:ts + SUBLANES, :]
    yb = proj(3) * conv

    merged = _sigmoid(proj(5)) * ya + _sigmoid(proj(6)) * yb
    o_ref[0] = x + _mm(merged.astype(BF16), wout_ref[...])


def _mixer(x, g1, w_in, caw, cab, wax, ba, bx, lam, cbw, w_out, ts):
    b, s, d = x.shape
    const = lambda shape: pl.BlockSpec(shape, lambda i, j: (0,) * len(shape))
    return pl.pallas_call(
        _mixer_kernel,
        grid=(b, s // ts),
        in_specs=[pl.BlockSpec((1, ts, d), lambda i, j: (i, j, 0)),
                  const(g1.shape), const(w_in.shape), const(caw.shape), const(cab.shape), const(wax.shape),
                  const(ba.shape), const(bx.shape), const(lam.shape), const(cbw.shape), const(w_out.shape)],
        out_specs=pl.BlockSpec((1, ts, d), lambda i, j: (i, j, 0)),
        out_shape=jax.ShapeDtypeStruct(x.shape, F32),
        scratch_shapes=[pltpu.VMEM((ts + SUBLANES, d), F32), pltpu.VMEM((ts + SUBLANES, d), F32),
                        pltpu.VMEM((ts, d), F32), pltpu.VMEM((ts, d), F32), pltpu.VMEM((ts, d), F32),
                        pltpu.VMEM((1, d), F32)],
        compiler_params=pltpu.CompilerParams(dimension_semantics=("arbitrary", "arbitrary"),
                                             vmem_limit_bytes=VMEM_LIMIT),
        name="mixer",
    )(x, g1, w_in, caw, cab, wax, ba, bx, lam, cbw, w_out)


def _extract_top(vals, keys, payload, n, key_sentinel):
    out_v, out_k, out_p = [], [], []
    neg_inf = jnp.float32(-jnp.inf)
    for _ in range(n):
        m = vals[0]
        for v in vals[1:]:
            m = jnp.maximum(m, v)
        m = jnp.max(m, axis=0, keepdims=True)
        cand = [jnp.where(v == m, k, key_sentinel) for v, k in zip(vals, keys)]
        am = cand[0]
        for c in cand[1:]:
            am = jnp.minimum(am, c)
        am = jnp.min(am, axis=0, keepdims=True)
        hit = [k == am for k in keys]
        out_v.append(m)
        out_k.append(am)
        if payload is not None:
            p = jnp.where(hit[0], payload[0], -1)
            for hh, pp in zip(hit[1:], payload[1:]):
                p = jnp.maximum(p, jnp.where(hh, pp, -1))
            out_p.append(jnp.max(p, axis=0, keepdims=True))
        vals = [jnp.where(hh, neg_inf, v) for hh, v in zip(hit, vals)]
    return out_v, out_k, out_p


def _route_kernel(h_ref, g2_ref, wq_ref, sk_ref, xn_ref, idx_ref, gate_ref, q_s, st_s, it_s, g_s):
    tt, d = h_ref.shape
    n_hp = sk_ref.shape[0]
    n_heads = n_hp // 2
    n_lt = tt // LANES

    xn = _rms(h_ref[...], g2_ref[...]).astype(BF16)
    xn_ref[...] = xn
    q = _mm(xn, wq_ref[...]).astype(BF16)
    for hp in range(n_hp):
        q_s[hp] = q[:, hp * LANES:(hp + 1) * LANES]

    sub = lax.broadcasted_iota(I32, (SUBLANES, LANES), 0)

    def stage1(hp, carry):
        sc = _mm_nt(sk_ref[hp], q_s[hp])
        for lt in range(n_lt):
            blk = [sc[j * SUBLANES:(j + 1) * SUBLANES, lt * LANES:(lt + 1) * LANES]
                   for j in range(NKEYS // SUBLANES)]
            keys = [sub + j * SUBLANES for j in range(NKEYS // SUBLANES)]
            v, k, _ = _extract_top(blk, keys, None, TOPK, NKEYS)
            st_s[hp, :, lt * LANES:(lt + 1) * LANES] = jnp.concatenate(v, axis=0)
            it_s[hp, :, lt * LANES:(lt + 1) * LANES] = jnp.concatenate(k, axis=0)
        return carry

    lax.fori_loop(0, n_hp, stage1, 0)

    big = TOPK * TOPK
    neg_inf = jnp.float32(-jnp.inf)

    def stage2(hh, carry):
        for lt in range(n_lt):
            ls = slice(lt * LANES, (lt + 1) * LANES)
            s0 = st_s[2 * hh, :, ls]
            s1 = st_s[2 * hh + 1, :, ls]
            i0 = it_s[2 * hh, :, ls] * (4 * NKEYS)
            i1 = it_s[2 * hh + 1, :, ls] * 4
            vals, keys, pay = [], [], []
            for i, jb, jmax in ((0, 0, 8), (0, 1, 8), (1, 0, 8), (2, 0, 5), (3, 0, 4)):
                v = s0[i:i + 1, :] + s1[jb * 8:(jb + 1) * 8, :]
                kk = sub + (i * TOPK + jb * 8)
                pp = i0[i:i + 1, :] + i1[jb * 8:(jb + 1) * 8, :]
                if jmax < 8:
                    ok = sub < jmax
                    v = jnp.where(ok, v, neg_inf)
                    kk = jnp.where(ok, kk, big)
                vals.append(v); keys.append(kk); pay.append(pp)
            for j, ib, lo, hi in ((0, 0, 4, 8), (0, 1, 0, 8), (1, 0, 4, 8), (2, 0, 4, 5)):
                v = s0[ib * 8:(ib + 1) * 8, :] + s1[j:j + 1, :]
                kk = (sub + ib * 8) * TOPK + j
                pp = i0[ib * 8:(ib + 1) * 8, :] + i1[j:j + 1, :]
                if lo > 0 or hi < 8:
                    ok = (sub >= lo) & (sub < hi)
                    v = jnp.where(ok, v, neg_inf)
                    kk = jnp.where(ok, kk, big)
                vals.append(v); keys.append(kk); pay.append(pp)
            bv, _, bp = _extract_top(vals, keys, pay, TOPK, big)
            best = jnp.concatenate(bv, axis=0)
            ex = jnp.exp(best - best[0:1, :])
            g_s[hh, :, ls] = ex / jnp.sum(ex, axis=0, keepdims=True)
            r0 = pl.multiple_of(hh * TOPK, TOPK)
            idx_ref[0, pl.ds(r0, TOPK), ls] = jnp.concatenate(bp, axis=0)
        return carry

    lax.fori_loop(0, n_heads, stage2, 0)

    gate_ref[...] = g_s[...].reshape(n_heads * TOPK, tt).T


def _route(h1, g2, wq, sk, tt):
    t, d = h1.shape
    n_hp = sk.shape[0]
    n_heads = n_hp // 2
    nsel = n_heads * TOPK
    const = lambda shape: pl.BlockSpec(shape, lambda i: (0,) * len(shape))
    return pl.pallas_call(
        _route_kernel,
        grid=(t // tt,),
        in_specs=[pl.BlockSpec((tt, d), lambda i: (i, 0)), const(g2.shape), const(wq.shape), const(sk.shape)],
        out_specs=[pl.BlockSpec((tt, d), lambda i: (i, 0)),
                   pl.BlockSpec((1, nsel, tt), lambda i: (i, 0, 0)),
                   pl.BlockSpec((tt, nsel), lambda i: (i, 0))],
        out_shape=[jax.ShapeDtypeStruct((t, d), BF16),
                   jax.ShapeDtypeStruct((t // tt, nsel, tt), I32),
                   jax.ShapeDtypeStruct((t, nsel), F32)],
        scratch_shapes=[pltpu.VMEM((n_hp, tt, LANES), BF16),
                        pltpu.VMEM((n_hp, TOPK, tt), F32), pltpu.VMEM((n_hp, TOPK, tt), I32),
                        pltpu.VMEM((n_heads, TOPK, tt), F32)],
        compiler_params=pltpu.CompilerParams(dimension_semantics=("arbitrary",),
                                             vmem_limit_bytes=VMEM_LIMIT),
        name="route",
    )(h1, g2, wq, sk)


def _two_stage_groups(n_groups, gather, finish, buf_a, buf_b):
    gather(0, buf_a)

    def body(m, carry):
        g0 = 2 * m
        finish(g0, buf_a)
        gather(g0 + 1, buf_b)
        finish(g0 + 1, buf_b)
        gather(jnp.minimum(g0 + 2, n_groups - 1), buf_a)
        return carry

    lax.fori_loop(0, n_groups // 2, body, 0)


def _gather_row(idx_ref, tab_ref, k, t, tb):
    return tab_ref[pl.ds(pl.multiple_of(idx_ref.at[pl.ds(k * tb, tb)][t], 4), 4), :]


def _peer_u_kernel(idx_ref, x_ref, gate_ref, sum_ref, tab_ref, o_ref, pa_s, pb_s):
    tb, nsel = gate_ref.shape

    def gather(grp, p_s):
        for g in range(PEER_GROUP):
            t = grp * PEER_GROUP + g
            xw = x_ref[pl.ds(pl.multiple_of(t * 4, 4), 4), :]
            xb = pltpu.bitcast(xw, BF16)
            ls = slice((g % 2) * LANES, (g % 2 + 1) * LANES)
            for k in range(nsel):
                row = _gather_row(idx_ref, tab_ref, k, t, tb)
                p_s[g // 2, k * SUBLANES:(k + 1) * SUBLANES, ls] = pltpu.bitcast(row, BF16) * xb

    def finish(grp, p_s):
        acts = []
        for g2 in range(PEER_GROUP // 2):
            r = _mm(sum_ref[...], p_s[g2])
            for half in range(2):
                rt = r[:, half * LANES:(half + 1) * LANES].T
                part = rt[0:SUBLANES, :]
                for j in range(1, LANES // SUBLANES):
                    part = part + rt[j * SUBLANES:(j + 1) * SUBLANES, :]
                acts.append(jnp.sum(part, axis=0, keepdims=True))
        r0 = pl.multiple_of(grp * PEER_GROUP, PEER_GROUP)
        o_ref[pl.ds(r0, PEER_GROUP), :] = (gate_ref[pl.ds(r0, PEER_GROUP), :]
                                           * _gelu(jnp.concatenate(acts, axis=0)))

    _two_stage_groups(tb // PEER_GROUP, gather, finish, pa_s, pb_s)


def _idx_spec(nsel, tb):
    return pl.BlockSpec((nsel * tb,), lambda i: (i,), memory_space=pltpu.SMEM, pipeline_mode=pl.Buffered(1))


def _peer_u(idx, xpk, gate, summat, table, tb):
    t, nsel = gate.shape
    return pl.pallas_call(
        _peer_u_kernel,
        grid=(t // tb,),
        in_specs=[_idx_spec(nsel, tb),
                  pl.BlockSpec((tb * 4, LANES), lambda i: (i, 0)),
                  pl.BlockSpec((tb, nsel), lambda i: (i, 0)),
                  pl.BlockSpec(memory_space=pltpu.VMEM),
                  pl.BlockSpec(memory_space=pltpu.VMEM)],
        out_specs=pl.BlockSpec((tb, nsel), lambda i: (i, 0)),
        out_shape=jax.ShapeDtypeStruct((t, nsel), F32),
        scratch_shapes=[pltpu.VMEM((PEER_GROUP // 2, nsel * SUBLANES, 2 * LANES), BF16),
                        pltpu.VMEM((PEER_GROUP // 2, nsel * SUBLANES, 2 * LANES), BF16)],
        compiler_params=pltpu.CompilerParams(dimension_semantics=("arbitrary",),
                                             vmem_limit_bytes=VMEM_LIMIT),
        name="peer_u",
    )(idx, xpk, gate, summat, table)


def _peer_v_kernel(idx_ref, w_ref, sum_ref, tab_ref, o_ref, pa_s, pb_s, wba_s, wbb_s):
    tb, nsel = w_ref.shape

    def gather(grp, bufs):
        p_s, wb_s = bufs
        r0 = pl.multiple_of(grp * PEER_GROUP, PEER_GROUP)
        wr = w_ref[pl.ds(r0, PEER_GROUP), :].astype(BF16).astype(F32)
        wbits = pltpu.bitcast(wr, I32)
        wdup = wbits | lax.shift_right_logical(wbits, 16)
        for g in range(PEER_GROUP):
            wb_s[g] = jnp.broadcast_to(wdup[g:g + 1, :], (nsel, nsel)).T
        for g in range(PEER_GROUP):
            t = grp * PEER_GROUP + g
            for k in range(nsel):
                row = _gather_row(idx_ref, tab_ref, k, t, tb)
                wk = jnp.broadcast_to(wb_s[g, k:k + 1, :], (4, LANES))
                p_s[g, k * SUBLANES:(k + 1) * SUBLANES, :] = pltpu.bitcast(row, BF16) * pltpu.bitcast(wk, BF16)

    def finish(grp, bufs):
        p_s, _ = bufs
        for g in range(PEER_GROUP):
            r0 = pl.multiple_of((grp * PEER_GROUP + g) * SUBLANES, SUBLANES)
            o_ref[pl.ds(r0, SUBLANES), :] = _mm(sum_ref[...], p_s[g])

    _two_stage_groups(tb // PEER_GROUP, gather, finish, (pa_s, wba_s), (pb_s, wbb_s))


def _peer_v(idx, w, summat, table, tb):
    t, nsel = w.shape
    return pl.pallas_call(
        _peer_v_kernel,
        grid=(t // tb,),
        in_specs=[_idx_spec(nsel, tb),
                  pl.BlockSpec((tb, nsel), lambda i: (i, 0)),
                  pl.BlockSpec(memory_space=pltpu.VMEM),
                  pl.BlockSpec(memory_space=pltpu.VMEM)],
        out_specs=pl.BlockSpec((tb * SUBLANES, LANES), lambda i: (i, 0)),
        out_shape=jax.ShapeDtypeStruct((t * SUBLANES, LANES), F32),
        scratch_shapes=[pltpu.VMEM((PEER_GROUP, nsel * SUBLANES, LANES), BF16),
                        pltpu.VMEM((PEER_GROUP, nsel * SUBLANES, LANES), BF16),
                        pltpu.VMEM((PEER_GROUP, nsel, nsel), I32),
                        pltpu.VMEM((PEER_GROUP, nsel, nsel), I32)],
        compiler_params=pltpu.CompilerParams(dimension_semantics=("arbitrary",),
                                             vmem_limit_bytes=VMEM_LIMIT),
        name="peer_v",
    )(idx, w, summat, table)


def _final_kernel(h_ref, p_ref, g_ref, o_ref):
    o_ref[...] = _rms(h_ref[...] + p_ref[...], g_ref[...])


def _final(h1, pout, g, tt):
    t, d = h1.shape
    return pl.pallas_call(
        _final_kernel,
        grid=(t // tt,),
        in_specs=[pl.BlockSpec((tt, d), lambda i: (i, 0)), pl.BlockSpec((tt, d), lambda i: (i, 0)),
                  pl.BlockSpec((1, d), lambda i: (0, 0))],
        out_specs=pl.BlockSpec((tt, d), lambda i: (i, 0)),
        out_shape=jax.ShapeDtypeStruct((t, d), F32),
        compiler_params=pltpu.CompilerParams(dimension_semantics=("arbitrary",)),
        name="final",
    )(h1, pout, g)


def _bf16_bits(a):
    return lax.bitcast_convert_type(a.astype(BF16), jnp.uint16).astype(jnp.uint32)


def _pack_sublane_pairs(a):
    n, d = a.shape
    bits = _bf16_bits(a).reshape(n, d // (2 * LANES), 2, LANES)
    words = bits[:, :, 0, :] | (bits[:, :, 1, :] << 16)
    return lax.bitcast_convert_type(words, I32).reshape(n * d // (2 * LANES), LANES)


def kernel(x, norm1_g, w_in, conv_a_w, conv_a_b, w_a, b_a, w_x, b_x, lru_lambda, conv_b_w, w_out, norm2_g,
           peer_wq, peer_subkeys, peer_u, peer_v, final_g):
    b, s, d = x.shape
    depth = norm1_g.shape[0]
    t = b * s
    h = x
    for l in range(depth):
        wax = jnp.concatenate([w_a[l], w_x[l]], axis=-1).astype(BF16)
        h = _mixer(h, norm1_g[l][None], w_in[l].astype(BF16), conv_a_w[l], conv_a_b[l][None], wax,
                   b_a[l][None], b_x[l][None], lru_lambda[l][None], conv_b_w[l], w_out[l].astype(BF16),
                   min(MIX_TS, s))
        h1 = h.reshape(t, d)
        n_heads, _, nkeys, half = peer_subkeys[l].shape
        sk = peer_subkeys[l].reshape(n_heads * 2, nkeys, half).astype(BF16)
        xn2, idx, gate = _route(h1, norm2_g[l][None], peer_wq[l].astype(BF16), sk, min(ROUTE_TT, t))
        nsel = n_heads * TOPK
        assert nsel == LANES and d == SUBLANES * LANES and nkeys == NKEYS and ROUTE_TT == PEER_TB
        idx = idx.reshape(-1)
        summat = (jnp.arange(nsel * SUBLANES)[None, :] // SUBLANES == jnp.arange(nsel)[:, None]).astype(BF16)
        sumchunk = (jnp.arange(nsel * SUBLANES)[None, :] % SUBLANES == jnp.arange(SUBLANES)[:, None]).astype(BF16)
        wts = _peer_u(idx, _pack_sublane_pairs(xn2), gate, summat, _pack_sublane_pairs(peer_u[l]),
                      min(PEER_TB, t))
        pout = _peer_v(idx, wts, sumchunk, _pack_sublane_pairs(peer_v[l]), min(PEER_TB, t)).reshape(t, d)
        if l + 1 < depth:
            h = (h1 + pout).reshape(b, s, d)
    return _final(h1, pout, final_g[None], min(ROUTE_TT, t)).reshape(b, s, d)
```

```python
import math

import jax
import jax.numpy as jnp
from jax import lax
from jax.experimental import pallas as pl
from jax.experimental.pallas import tpu as pltpu

F32 = jnp.float32
BF16 = jnp.bfloat16
I32 = jnp.int32

EPS = 1e-6
LRU_C = 8.0
LANES = 128
SUBLANES = 8
TOPK = 16
NKEYS = 128
VMEM_LIMIT = 56 * 1024 * 1024

MIX_TS = 256
ROUTE_TT = 256
ROUTE_HEAD_UNROLL = 2
ROUTE_HP_UNROLL = 4
PEER_TB = 256
PACK_ROWS = 512
PEER_GROUP = 8


def _gelu(x):
    return 0.5 * x * (1.0 + jnp.tanh(math.sqrt(2.0 / math.pi) * (x + 0.044715 * (x * x * x))))


def _sigmoid(x):
    return 0.5 * (jnp.tanh(0.5 * x) + 1.0)


def _mm(a, b):
    return jnp.dot(a, b, preferred_element_type=F32)


def _mm_nt(a, b):
    return lax.dot_general(a, b, (((1,), (1,)), ((), ())), preferred_element_type=F32)


def _rms(x, g):
    return x * lax.rsqrt(jnp.mean(x * x, axis=-1, keepdims=True) + EPS) * g


def _mixer_kernel(x_ref, g1_ref, win_ref, caw_ref, cab_ref, wax_ref, ba_ref, bx_ref, lam_ref, cbw_ref,
                  wout_ref, o_ref, xa_ext, cv_ext, a_s, u_s, h_s, hc_s):
    ts, d = a_s.shape
    hd = wax_ref.shape[1]
    n_heads = wax_ref.shape[0]

    @pl.when(pl.program_id(1) == 0)
    def _():
        xa_ext[0:SUBLANES, :] = jnp.zeros((SUBLANES, d), F32)
        cv_ext[0:SUBLANES, :] = jnp.zeros((SUBLANES, d), F32)
        hc_s[...] = jnp.zeros_like(hc_s)

    x = x_ref[0]
    xn = _rms(x, g1_ref[...]).astype(BF16)

    def proj(j):
        return _mm(xn, win_ref[:, j * d:(j + 1) * d])

    xa_ext[SUBLANES:SUBLANES + ts, :] = proj(0)
    xc = cab_ref[...]
    for k in range(4):
        xc = xc + caw_ref[k:k + 1, :] * xa_ext[SUBLANES - 3 + k:SUBLANES - 3 + k + ts, :]
    xa_ext[0:SUBLANES, :] = xa_ext[ts:ts + SUBLANES, :]

    lam = lam_ref[...]
    neg_c_sp = -LRU_C * (jnp.maximum(-lam, 0.0) + jnp.log1p(jnp.exp(-jnp.abs(lam))))
    xcb = xc.astype(BF16)
    for h in range(n_heads):
        sl = slice(h * hd, (h + 1) * hd)
        gates = _mm(xcb[:, sl], wax_ref[h])
        r = _sigmoid(gates[:, :hd] + ba_ref[:, sl])
        i = _sigmoid(gates[:, hd:] + bx_ref[:, sl])
        log_a = r * neg_c_sp[:, sl]
        a = jnp.exp(log_a)
        mult = jnp.sqrt(1.0 - a * a)
        a_s[:, sl] = a
        u_s[:, sl] = mult * (i * xc[:, sl])

    row = lax.broadcasted_iota(I32, (SUBLANES, d), 0)

    def scan_body(g, hprev):
        r0 = pl.multiple_of(g * SUBLANES, SUBLANES)
        av = a_s[pl.ds(r0, SUBLANES), :]
        uv = u_s[pl.ds(r0, SUBLANES), :]
        for sh in (1, 2, 4):
            keep = row >= sh
            a_sh = jnp.where(keep, pltpu.roll(av, sh, axis=0), 1.0)
            u_sh = jnp.where(keep, pltpu.roll(uv, sh, axis=0), 0.0)
            uv = uv + av * u_sh
            av = av * a_sh
        hv = uv + av * hprev
        h_s[pl.ds(r0, SUBLANES), :] = hv
        return hv[SUBLANES - 1:SUBLANES, :]

    hc_s[...] = lax.fori_loop(0, ts // SUBLANES, scan_body, hc_s[...])

    ya = h_s[...] * _gelu(proj(1))

    cv_ext[SUBLANES:SUBLANES + ts, :] = proj(4) * proj(2)
    conv = cbw_ref[0:1, :] * cv_ext[SUBLANES - 2:SUBLANES - 2 + ts, :]
    for k in range(1, 3):
        conv = conv + cbw_ref[k:k + 1, :] * cv_ext[SUBLANES - 2 + k:SUBLANES - 2 + k + ts, :]
    cv_ext[0:SUBLANES, :] = cv_ext[ts:ts + SUBLANES, :]
    yb = proj(3) * conv

    merged = _sigmoid(proj(5)) * ya + _sigmoid(proj(6)) * yb
    o_ref[0] = x + _mm(merged.astype(BF16), wout_ref[...])


def _mixer(x, g1, w_in, caw, cab, wax, ba, bx, lam, cbw, w_out, ts):
    b, s, d = x.shape
    const = lambda shape: pl.BlockSpec(shape, lambda i, j: (0,) * len(shape))
    return pl.pallas_call(
        _mixer_kernel,
        grid=(b, s // ts),
        in_specs=[pl.BlockSpec((1, ts, d), lambda i, j: (i, j, 0)),
                  const(g1.shape), const(w_in.shape), const(caw.shape), const(cab.shape), const(wax.shape),
                  const(ba.shape), const(bx.shape), const(lam.shape), const(cbw.shape), const(w_out.shape)],
        out_specs=pl.BlockSpec((1, ts, d), lambda i, j: (i, j, 0)),
        out_shape=jax.ShapeDtypeStruct(x.shape, F32),
        scratch_shapes=[pltpu.VMEM((ts + SUBLANES, d), F32), pltpu.VMEM((ts + SUBLANES, d), F32),
                        pltpu.VMEM((ts, d), F32), pltpu.VMEM((ts, d), F32), pltpu.VMEM((ts, d), F32),
                        pltpu.VMEM((1, d), F32)],
        compiler_params=pltpu.CompilerParams(dimension_semantics=("arbitrary", "arbitrary"),
                                             vmem_limit_bytes=VMEM_LIMIT),
        name="mixer",
    )(x, g1, w_in, caw, cab, wax, ba, bx, lam, cbw, w_out)


def _extract_top(vals, keys, payload, n, key_sentinel):
    out_v, out_k, out_p = [], [], []
    neg_inf = jnp.float32(-jnp.inf)
    for _ in range(n):
        m = vals[0]
        for v in vals[1:]:
            m = jnp.maximum(m, v)
        m = jnp.max(m, axis=0, keepdims=True)
        cand = [jnp.where(v == m, k, key_sentinel) for v, k in zip(vals, keys)]
        am = cand[0]
        for c in cand[1:]:
            am = jnp.minimum(am, c)
        am = jnp.min(am, axis=0, keepdims=True)
        hit = [k == am for k in keys]
        out_v.append(m)
        out_k.append(am)
        if payload is not None:
            p = jnp.where(hit[0], payload[0], -1)
            for hh, pp in zip(hit[1:], payload[1:]):
                p = jnp.maximum(p, jnp.where(hh, pp, -1))
            out_p.append(jnp.max(p, axis=0, keepdims=True))
        vals = [jnp.where(hh, neg_inf, v) for hh, v in zip(hit, vals)]
    return out_v, out_k, out_p


def _pack_rows_store(x, out_ref):
    n, d = x.shape
    n_words = d // (2 * LANES)
    bits = pltpu.bitcast(x, I32)
    for s in range(n_words):
        lo = bits[:, (2 * s) * LANES:(2 * s + 1) * LANES]
        hi = bits[:, (2 * s + 1) * LANES:(2 * s + 2) * LANES]
        out_ref[pl.ds(s, n, stride=n_words), :] = lax.shift_right_logical(lo, 16) | (hi & jnp.int32(-65536))


def _route_kernel(h_ref, g2_ref, wq_ref, sk_ref, xpk_ref, idx_ref, gate_ref, q_s, st_s, it_s, g_s):
    tt, d = h_ref.shape
    n_hp = sk_ref.shape[0]
    n_heads = n_hp // 2
    n_lt = tt // LANES

    xn = _rms(h_ref[...], g2_ref[...]).astype(BF16)
    _pack_rows_store(xn.astype(F32), xpk_ref)
    q = _mm(xn, wq_ref[...]).astype(BF16)
    for hp in range(n_hp):
        q_s[hp] = q[:, hp * LANES:(hp + 1) * LANES]

    sub = lax.broadcasted_iota(I32, (SUBLANES, LANES), 0)

    def stage1(it, carry):
        for u in range(ROUTE_HP_UNROLL):
            hp = it * ROUTE_HP_UNROLL + u
            sc = _mm_nt(sk_ref[hp], q_s[hp])
            for lt in range(n_lt):
                blk = [sc[j * SUBLANES:(j + 1) * SUBLANES, lt * LANES:(lt + 1) * LANES]
                       for j in range(NKEYS // SUBLANES)]
                keys = [sub + j * SUBLANES for j in range(NKEYS // SUBLANES)]
                v, k, _ = _extract_top(blk, keys, None, TOPK, NKEYS)
                st_s[hp, :, lt * LANES:(lt + 1) * LANES] = jnp.concatenate(v, axis=0)
                it_s[hp, :, lt * LANES:(lt + 1) * LANES] = jnp.concatenate(k, axis=0)
        return carry

    lax.fori_loop(0, n_hp // ROUTE_HP_UNROLL, stage1, 0)

    big = TOPK * TOPK
    neg_inf = jnp.float32(-jnp.inf)

    def stage2(it, carry):
        for hh, lt in [(it * ROUTE_HEAD_UNROLL + u, lt) for u in range(ROUTE_HEAD_UNROLL) for lt in range(n_lt)]:
            ls = slice(lt * LANES, (lt + 1) * LANES)
            s0 = st_s[2 * hh, :, ls]
            s1 = st_s[2 * hh + 1, :, ls]
            i0 = it_s[2 * hh, :, ls] * (4 * NKEYS)
            i1 = it_s[2 * hh + 1, :, ls] * 4
            vals, keys, pay = [], [], []
            for i, jb, jmax in ((0, 0, 8), (0, 1, 8), (1, 0, 8), (2, 0, 5), (3, 0, 4)):
                v = s0[i:i + 1, :] + s1[jb * 8:(jb + 1) * 8, :]
                kk = sub + (i * TOPK + jb * 8)
                pp = i0[i:i + 1, :] + i1[jb * 8:(jb + 1) * 8, :]
                if jmax < 8:
                    ok = sub < jmax
                    v = jnp.where(ok, v, neg_inf)
                    kk = jnp.where(ok, kk, big)
                vals.append(v); keys.append(kk); pay.append(pp)
            for j, ib, lo, hi in ((0, 0, 4, 8), (0, 1, 0, 8), (1, 0, 4, 8), (2, 0, 4, 5)):
                v = s0[ib * 8:(ib + 1) * 8, :] + s1[j:j + 1, :]
                kk = (sub + ib * 8) * TOPK + j
                pp = i0[ib * 8:(ib + 1) * 8, :] + i1[j:j + 1, :]
                if lo > 0 or hi < 8:
                    ok = (sub >= lo) & (sub < hi)
                    v = jnp.where(ok, v, neg_inf)
                    kk = jnp.where(ok, kk, big)
                vals.append(v); keys.append(kk); pay.append(pp)
            bv, _, bp = _extract_top(vals, keys, pay, TOPK, big)
            best = jnp.concatenate(bv, axis=0)
            ex = jnp.exp(best - best[0:1, :])
            g_s[hh, :, ls] = ex / jnp.sum(ex, axis=0, keepdims=True)
            r0 = pl.multiple_of(hh * TOPK, TOPK)
            idx_ref[0, pl.ds(r0, TOPK), ls] = jnp.concatenate(bp, axis=0)
        return carry

    lax.fori_loop(0, n_heads // ROUTE_HEAD_UNROLL, stage2, 0)

    gate_ref[...] = g_s[...].reshape(n_heads * TOPK, tt).T


def _route(h1, g2, wq, sk, tt):
    t, d = h1.shape
    n_hp = sk.shape[0]
    n_heads = n_hp // 2
    nsel = n_heads * TOPK
    const = lambda shape: pl.BlockSpec(shape, lambda i: (0,) * len(shape))
    return pl.pallas_call(
        _route_kernel,
        grid=(t // tt,),
        in_specs=[pl.BlockSpec((tt, d), lambda i: (i, 0)), const(g2.shape), const(wq.shape), const(sk.shape)],
        out_specs=[pl.BlockSpec((tt * d // (2 * LANES), LANES), lambda i: (i, 0)),
                   pl.BlockSpec((1, nsel, tt), lambda i: (i, 0, 0)),
                   pl.BlockSpec((tt, nsel), lambda i: (i, 0))],
        out_shape=[jax.ShapeDtypeStruct((t * d // (2 * LANES), LANES), I32),
                   jax.ShapeDtypeStruct((t // tt, nsel, tt), I32),
                   jax.ShapeDtypeStruct((t, nsel), F32)],
        scratch_shapes=[pltpu.VMEM((n_hp, tt, LANES), BF16),
                        pltpu.VMEM((n_hp, TOPK, tt), F32), pltpu.VMEM((n_hp, TOPK, tt), I32),
                        pltpu.VMEM((n_heads, TOPK, tt), F32)],
        compiler_params=pltpu.CompilerParams(dimension_semantics=("arbitrary",),
                                             vmem_limit_bytes=VMEM_LIMIT),
        name="route",
    )(h1, g2, wq, sk)


def _two_stage_groups(n_groups, gather, finish, buf_a, buf_b):
    gather(0, buf_a)

    def body(m, carry):
        g0 = 2 * m
        finish(g0, buf_a)
        gather(g0 + 1, buf_b)
        finish(g0 + 1, buf_b)
        gather(jnp.minimum(g0 + 2, n_groups - 1), buf_a)
        return carry

    lax.fori_loop(0, n_groups // 2, body, 0)


def _idx_block_copy(idx_hbm, idx_s, sem, step, slot, blk):
    src = idx_hbm.at[pl.ds(pl.multiple_of(step * blk, blk), blk)]
    dst = idx_s.at[pl.ds(pl.multiple_of(slot * blk, blk), blk)]
    return pltpu.make_async_copy(src, dst, sem.at[slot])


def _idx_prefetch(idx_hbm, idx_s, sem, blk):
    i = pl.program_id(0)
    slot = lax.rem(i, 2)

    @pl.when(i == 0)
    def _():
        _idx_block_copy(idx_hbm, idx_s, sem, 0, 0, blk).start()

    _idx_block_copy(idx_hbm, idx_s, sem, i, slot, blk).wait()

    @pl.when(i + 1 < pl.num_programs(0))
    def _():
        _idx_block_copy(idx_hbm, idx_s, sem, i + 1, 1 - slot, blk).start()

    return slot * blk


def _gather_row(idx_s, tab_ref, k, off_t, tb, blk):
    return tab_ref[pl.ds(pl.multiple_of(idx_s.at[pl.ds(k * tb, blk + tb)][off_t], 4), 4), :]


def _peer_u_kernel(idx_hbm, x_ref, gate_ref, sum_ref, tab_ref, o_ref, pa_s, pb_s, idx_s, sem):
    tb, nsel = gate_ref.shape
    blk = nsel * tb
    off = _idx_prefetch(idx_hbm, idx_s, sem, blk)

    def gather(grp, p_s):
        for g in range(PEER_GROUP):
            t = grp * PEER_GROUP + g
            xw = x_ref[pl.ds(pl.multiple_of(t * 4, 4), 4), :]
            xb = pltpu.bitcast(xw, BF16)
            ls = slice((g % 2) * LANES, (g % 2 + 1) * LANES)
            for k in range(nsel):
                row = _gather_row(idx_s, tab_ref, k, off + t, tb, blk)
                p_s[g // 2, k * SUBLANES:(k + 1) * SUBLANES, ls] = pltpu.bitcast(row, BF16) * xb

    def finish(grp, p_s):
        acts = []
        for g2 in range(PEER_GROUP // 2):
            r = _mm(sum_ref[...], p_s[g2])
            for half in range(2):
                rt = r[:, half * LANES:(half + 1) * LANES].T
                part = rt[0:SUBLANES, :]
                for j in range(1, LANES // SUBLANES):
                    part = part + rt[j * SUBLANES:(j + 1) * SUBLANES, :]
                acts.append(jnp.sum(part, axis=0, keepdims=True))
        r0 = pl.multiple_of(grp * PEER_GROUP, PEER_GROUP)
        o_ref[pl.ds(r0, PEER_GROUP), :] = (gate_ref[pl.ds(r0, PEER_GROUP), :]
                                           * _gelu(jnp.concatenate(acts, axis=0)))

    _two_stage_groups(tb // PEER_GROUP, gather, finish, pa_s, pb_s)


def _idx_scratch(nsel, tb):
    return [pltpu.SMEM((2 * nsel * tb,), I32), pltpu.SemaphoreType.DMA((2,))]


def _peer_u(idx, xpk, gate, summat, table, tb):
    t, nsel = gate.shape
    return pl.pallas_call(
        _peer_u_kernel,
        grid=(t // tb,),
        in_specs=[pl.BlockSpec(memory_space=pl.ANY),
                  pl.BlockSpec((tb * 4, LANES), lambda i: (i, 0)),
                  pl.BlockSpec((tb, nsel), lambda i: (i, 0)),
                  pl.BlockSpec(memory_space=pltpu.VMEM),
                  pl.BlockSpec(memory_space=pltpu.VMEM)],
        out_specs=pl.BlockSpec((tb, nsel), lambda i: (i, 0)),
        out_shape=jax.ShapeDtypeStruct((t, nsel), F32),
        scratch_shapes=[pltpu.VMEM((PEER_GROUP // 2, nsel * SUBLANES, 2 * LANES), BF16),
                        pltpu.VMEM((PEER_GROUP // 2, nsel * SUBLANES, 2 * LANES), BF16)] + _idx_scratch(nsel, tb),
        compiler_params=pltpu.CompilerParams(dimension_semantics=("arbitrary",),
                                             vmem_limit_bytes=VMEM_LIMIT),
        name="peer_u",
    )(idx, xpk, gate, summat, table)


def _peer_v_kernel(idx_hbm, w_ref, sum_ref, tab_ref, o_ref, pa_s, pb_s, wba_s, wbb_s, idx_s, sem):
    tb, nsel = w_ref.shape
    blk = nsel * tb
    off = _idx_prefetch(idx_hbm, idx_s, sem, blk)

    def gather(grp, bufs):
        p_s, wb_s = bufs
        r0 = pl.multiple_of(grp * PEER_GROUP, PEER_GROUP)
        wr = w_ref[pl.ds(r0, PEER_GROUP), :].astype(BF16).astype(F32)
        wbits = pltpu.bitcast(wr, I32)
        wdup = wbits | lax.shift_right_logical(wbits, 16)
        for g in range(PEER_GROUP):
            wb_s[g] = jnp.broadcast_to(wdup[g:g + 1, :], (nsel, nsel)).T
        for g in range(PEER_GROUP):
            t = grp * PEER_GROUP + g
            for k in range(nsel):
                row = _gather_row(idx_s, tab_ref, k, off + t, tb, blk)
                wk =jnp.broadcast_to(wb_s[g, k:k + 1, :], (4, LANES))
                p_s[g, k * SUBLANES:(k + 1) * SUBLANES, :] = pltpu.bitcast(row, BF16) * pltpu.bitcast(wk, BF16)

    def finish(grp, bufs):
        p_s, _ = bufs
        for g in range(PEER_GROUP):
            r0 = pl.multiple_of((grp * PEER_GROUP + g) * SUBLANES, SUBLANES)
            o_ref[pl.ds(r0, SUBLANES), :] = _mm(sum_ref[...], p_s[g])

    _two_stage_groups(tb // PEER_GROUP, gather, finish, (pa_s, wba_s), (pb_s, wbb_s))


def _peer_v(idx, w, summat, table, tb):
    t, nsel = w.shape
    return pl.pallas_call(
        _peer_v_kernel,
        grid=(t // tb,),
        in_specs=[pl.BlockSpec(memory_space=pl.ANY),
                  pl.BlockSpec((tb, nsel), lambda i: (i, 0)),
                  pl.BlockSpec(memory_space=pltpu.VMEM),
                  pl.BlockSpec(memory_space=pltpu.VMEM)],
        out_specs=pl.BlockSpec((tb * SUBLANES, LANES), lambda i: (i, 0)),
        out_shape=jax.ShapeDtypeStruct((t * SUBLANES, LANES), F32),
        scratch_shapes=[pltpu.VMEM((PEER_GROUP, nsel * SUBLANES, LANES), BF16),
                        pltpu.VMEM((PEER_GROUP, nsel * SUBLANES, LANES), BF16),
                        pltpu.VMEM((PEER_GROUP, nsel, nsel), I32),
                        pltpu.VMEM((PEER_GROUP, nsel, nsel), I32)] + _idx_scratch(nsel, tb),
        compiler_params=pltpu.CompilerParams(dimension_semantics=("arbitrary",),
                                             vmem_limit_bytes=VMEM_LIMIT),
        name="peer_v",
    )(idx, w, summat, table)


def _final_kernel(h_ref, p_ref, g_ref, o_ref):
    tt, d = h_ref.shape
    n_chunks = d // LANES
    p = jnp.concatenate([p_ref[pl.ds(c, tt, stride=n_chunks), :] for c in range(n_chunks)], axis=1)
    o_ref[...] = _rms(h_ref[...] + p, g_ref[...])


def _final(h1, pout_fold, g, tt):
    t, d = h1.shape
    return pl.pallas_call(
        _final_kernel,
        grid=(t // tt,),
        in_specs=[pl.BlockSpec((tt, d), lambda i: (i, 0)),
                  pl.BlockSpec((tt * d // LANES, LANES), lambda i: (i, 0)),
                  pl.BlockSpec((1, d), lambda i: (0, 0))],
        out_specs=pl.BlockSpec((tt, d), lambda i: (i, 0)),
        out_shape=jax.ShapeDtypeStruct((t, d), F32),
        compiler_params=pltpu.CompilerParams(dimension_semantics=("arbitrary",)),
        name="final",
    )(h1, pout_fold, g)


def _pack_table_kernel(t_ref, o_ref):
    _pack_rows_store(t_ref[...].astype(BF16).astype(F32), o_ref)


def _pack_table(table, rows):
    n, d = table.shape
    return pl.pallas_call(
        _pack_table_kernel,
        grid=(n // rows,),
        in_specs=[pl.BlockSpec((rows, d), lambda i: (i, 0))],
        out_specs=pl.BlockSpec((rows * d // (2 * LANES), LANES), lambda i: (i, 0)),
        out_shape=jax.ShapeDtypeStruct((n * d // (2 * LANES), LANES), I32),
        compiler_params=pltpu.CompilerParams(dimension_semantics=("arbitrary",)),
        name="pack_table",
    )(table)


def kernel(x, norm1_g, w_in, conv_a_w, conv_a_b, w_a, b_a, w_x, b_x, lru_lambda, conv_b_w, w_out, norm2_g,
           peer_wq, peer_subkeys, peer_u, peer_v, final_g):
    b, s, d = x.shape
    depth = norm1_g.shape[0]
    t = b * s
    h = x
    for l in range(depth):
        wax = jnp.concatenate([w_a[l], w_x[l]], axis=-1).astype(BF16)
        h = _mixer(h, norm1_g[l][None], w_in[l].astype(BF16), conv_a_w[l], conv_a_b[l][None], wax,
                   b_a[l][None], b_x[l][None], lru_lambda[l][None], conv_b_w[l], w_out[l].astype(BF16),
                   min(MIX_TS, s))
        h1 = h.reshape(t, d)
        n_heads, _, nkeys, half = peer_subkeys[l].shape
        sk = peer_subkeys[l].reshape(n_heads * 2, nkeys, half).astype(BF16)
        xpk, idx, gate = _route(h1, norm2_g[l][None], peer_wq[l].astype(BF16), sk, min(ROUTE_TT, t))
        nsel = n_heads * TOPK
        assert nsel == LANES and d == SUBLANES * LANES and nkeys == NKEYS and ROUTE_TT == PEER_TB
        idx = idx.reshape(-1)
        summat = (jnp.arange(nsel * SUBLANES)[None, :] // SUBLANES == jnp.arange(nsel)[:, None]).astype(BF16)
        sumchunk = (jnp.arange(nsel * SUBLANES)[None, :] % SUBLANES == jnp.arange(SUBLANES)[:, None]).astype(BF16)
        wts = _peer_u(idx, xpk, gate, summat, _pack_table(peer_u[l], PACK_ROWS), min(PEER_TB, t))
        pout_fold = _peer_v(idx, wts, sumchunk, _pack_table(peer_v[l], PACK_ROWS), min(PEER_TB, t))
        if l + 1 < depth:
            h = (h1 + pout_fold.reshape(t, d)).reshape(b, s, d)
    return _final(h1, pout_fold, final_g[None], min(ROUTE_TT, t)).reshape(b, s, d)
```

```python
import math

import jax
import jax.numpy as jnp
from jax import lax
from jax.experimental import pallas as pl
from jax.experimental.pallas import tpu as pltpu

F32 = jnp.float32
BF16 = jnp.bfloat16
I32 = jnp.int32

EPS = 1e-6
LRU_C = 8.0
LANES = 128
SUBLANES = 8
TOPK = 16
NKEYS = 128
VMEM_LIMIT = 56 * 1024 * 1024

MIX_TS = 256
ROUTE_TT = 512
ROUTE_HEAD_UNROLL = 2
ROUTE_HP_UNROLL = 2
PEER_TB = 512
PACK_ROWS = 512
PEER_GROUP = 8
PEER_V_BF16_RUN = 4
PEER_BODY_PAIRS = 1


def _gelu(x):
    return 0.5 * x * (1.0 + jnp.tanh(math.sqrt(2.0 / math.pi) * (x + 0.044715 * (x * x * x))))


def _sigmoid(x):
    return 0.5 * (jnp.tanh(0.5 * x) + 1.0)


def _mm(a, b):
    return jnp.dot(a, b, preferred_element_type=F32)


def _mm_nt(a, b):
    return lax.dot_general(a, b, (((1,), (1,)), ((), ())), preferred_element_type=F32)


def _rms(x, g):
    return x * lax.rsqrt(jnp.mean(x * x, axis=-1, keepdims=True) + EPS) * g


def _mixer_kernel(x_ref, g1_ref, win_ref, caw_ref, cab_ref, wax_ref, ba_ref, bx_ref, lam_ref, cbw_ref,
                  wout_ref, o_ref, xa_ext, cv_ext, a_s, u_s, h_s, hc_s):
    ts, d = a_s.shape
    hd = wax_ref.shape[1]
    n_heads = wax_ref.shape[0]

    @pl.when(pl.program_id(1) == 0)
    def _():
        xa_ext[0:SUBLANES, :] = jnp.zeros((SUBLANES, d), F32)
        cv_ext[0:SUBLANES, :] = jnp.zeros((SUBLANES, d), F32)
        hc_s[...] = jnp.zeros_like(hc_s)

    x = x_ref[0]
    xn = _rms(x, g1_ref[...]).astype(BF16)

    def proj(j):
        return _mm(xn, win_ref[:, j * d:(j + 1) * d])

    xa_ext[SUBLANES:SUBLANES + ts, :] = proj(0)
    xc = cab_ref[...]
    for k in range(4):
        xc = xc + caw_ref[k:k + 1, :] * xa_ext[SUBLANES - 3 + k:SUBLANES - 3 + k + ts, :]
    xa_ext[0:SUBLANES, :] = xa_ext[ts:ts + SUBLANES, :]

    lam = lam_ref[...]
    neg_c_sp = -LRU_C * (jnp.maximum(-lam, 0.0) + jnp.log1p(jnp.exp(-jnp.abs(lam))))
    xcb = xc.astype(BF16)
    for h in range(n_heads):
        sl = slice(h * hd, (h + 1) * hd)
        gates = _mm(xcb[:, sl], wax_ref[h])
        r = _sigmoid(gates[:, :hd] + ba_ref[:, sl])
        i = _sigmoid(gates[:, hd:] + bx_ref[:, sl])
        log_a = r * neg_c_sp[:, sl]
        a = jnp.exp(log_a)
        mult = jnp.sqrt(1.0 - a * a)
        a_s[:, sl] = a
        u_s[:, sl] = mult * (i * xc[:, sl])

    row = lax.broadcasted_iota(I32, (SUBLANES, d), 0)

    def scan_body(g, hprev):
        r0 = g * SUBLANES
        av = a_s[pl.ds(r0, SUBLANES), :]
        uv = u_s[pl.ds(r0, SUBLANES), :]
        for sh in (1, 2, 4):
            keep = row >= sh
            a_sh = jnp.where(keep, pltpu.roll(av, sh, axis=0), 1.0)
            u_sh = jnp.where(keep, pltpu.roll(uv, sh, axis=0), 0.0)
            uv = uv + av * u_sh
            av = av * a_sh
        hv = uv + av * hprev
        h_s[pl.ds(r0, SUBLANES), :] = hv
        return hv[SUBLANES - 1:SUBLANES, :]

    hprev = hc_s[...]
    for g in range(ts // SUBLANES):
        hprev = scan_body(g, hprev)
    hc_s[...] = hprev

    ya = h_s[...] * _gelu(proj(1))

    cv_ext[SUBLANES:SUBLANES + ts, :] = proj(4) * proj(2)
    conv = cbw_ref[0:1, :] * cv_ext[SUBLANES - 2:SUBLANES - 2 + ts, :]
    for k in range(1, 3):
        conv = conv + cbw_ref[k:k + 1, :] * cv_ext[SUBLANES - 2 + k:SUBLANES - 2 + k + ts, :]
    cv_ext[0:SUBLANES, :] = cv_ext[ts:ts + SUBLANES, :]
    yb = proj(3) * conv

    merged = _sigmoid(proj(5)) * ya + _sigmoid(proj(6)) * yb
    o_ref[0] = x + _mm(merged.astype(BF16), wout_ref[...])


def _mixer(x, g1, w_in, caw, cab, wax, ba, bx, lam, cbw, w_out, ts):
    b, s, d = x.shape
    const = lambda shape: pl.BlockSpec(shape, lambda i, j: (0,) * len(shape))
    return pl.pallas_call(
        _mixer_kernel,
        grid=(b, s // ts),
        in_specs=[pl.BlockSpec((1, ts, d), lambda i, j: (i, j, 0)),
                  const(g1.shape), const(w_in.shape), const(caw.shape), const(cab.shape), const(wax.shape),
                  const(ba.shape), const(bx.shape), const(lam.shape), const(cbw.shape), const(w_out.shape)],
        out_specs=pl.BlockSpec((1, ts, d), lambda i, j: (i, j, 0)),
        out_shape=jax.ShapeDtypeStruct(x.shape, F32),
        scratch_shapes=[pltpu.VMEM((ts + SUBLANES, d), F32), pltpu.VMEM((ts + SUBLANES, d), F32),
                        pltpu.VMEM((ts, d), F32), pltpu.VMEM((ts, d), F32), pltpu.VMEM((ts, d), F32),
                        pltpu.VMEM((1, d), F32)],
        compiler_params=pltpu.CompilerParams(dimension_semantics=("arbitrary", "arbitrary"),
                                             vmem_limit_bytes=VMEM_LIMIT),
        name="mixer",
    )(x, g1, w_in, caw, cab, wax, ba, bx, lam, cbw, w_out)


def _extract_top(vals, keys, payload, n, key_sentinel):
    out_v, out_k, out_p = [], [], []
    neg_inf = jnp.float32(-jnp.inf)
    for _ in range(n):
        m = vals[0]
        for v in vals[1:]:
            m = jnp.maximum(m, v)
        m = jnp.max(m, axis=0, keepdims=True)
        cand = [jnp.where(v == m, k, key_sentinel) for v, k in zip(vals, keys)]
        am = cand[0]
        for c in cand[1:]:
            am = jnp.minimum(am, c)
        am = jnp.min(am, axis=0, keepdims=True)
        hit = [k == am for k in keys]
        out_v.append(m)
        out_k.append(am)
        if payload is not None:
            p = jnp.where(hit[0], payload[0], -1)
            for hh, pp in zip(hit[1:], payload[1:]):
                p = jnp.maximum(p, jnp.where(hh, pp, -1))
            out_p.append(jnp.max(p, axis=0, keepdims=True))
        vals = [jnp.where(hh, neg_inf, v) for hh, v in zip(hit, vals)]
    return out_v, out_k, out_p


def _pack_rows_store(x, out_ref):
    n, d = x.shape
    n_words = d // (2 * LANES)
    bits = pltpu.bitcast(x, I32)
    for s in range(n_words):
        lo = bits[:, (2 * s) * LANES:(2 * s + 1) * LANES]
        hi = bits[:, (2 * s + 1) * LANES:(2 * s + 2) * LANES]
        out_ref[pl.ds(s, n, stride=n_words), :] = lax.shift_right_logical(lo, 16) | (hi & jnp.int32(-65536))


def _route_kernel(h_ref, g2_ref, wq_ref, sk_ref, xpk_ref, idx_ref, gate_ref, q_s, st_s, it_s, g_s):
    tt, d = h_ref.shape
    n_hp = sk_ref.shape[0]
    n_heads = n_hp // 2
    n_lt = tt // LANES

    xn = _rms(h_ref[...], g2_ref[...]).astype(BF16)
    _pack_rows_store(xn.astype(F32), xpk_ref)
    q = _mm(xn, wq_ref[...]).astype(BF16)
    for hp in range(n_hp):
        q_s[hp] = q[:, hp * LANES:(hp + 1) * LANES]

    sub = lax.broadcasted_iota(I32, (SUBLANES, LANES), 0)

    def stage1(it, carry):
        for u in range(ROUTE_HP_UNROLL):
            hp = it * ROUTE_HP_UNROLL + u
            sc = _mm_nt(sk_ref[hp], q_s[hp])
            for lt in range(n_lt):
                blk = [sc[j * SUBLANES:(j + 1) * SUBLANES, lt * LANES:(lt + 1) * LANES]
                       for j in range(NKEYS // SUBLANES)]
                keys = [sub + j * SUBLANES for j in range(NKEYS // SUBLANES)]
                v, k, _ = _extract_top(blk, keys, None, TOPK, NKEYS)
                st_s[hp, :, lt * LANES:(lt + 1) * LANES] = jnp.concatenate(v, axis=0)
                it_s[hp, :, lt * LANES:(lt + 1) * LANES] = jnp.concatenate(k, axis=0)
        return carry

    lax.fori_loop(0, n_hp // ROUTE_HP_UNROLL, stage1, 0)

    big = TOPK * TOPK
    neg_inf = jnp.float32(-jnp.inf)

    def stage2(it, carry):
        for hh, lt in [(it * ROUTE_HEAD_UNROLL + u, lt) for u in range(ROUTE_HEAD_UNROLL) for lt in range(n_lt)]:
            ls = slice(lt * LANES, (lt + 1) * LANES)
            s0 = st_s[2 * hh, :, ls]
            s1 = st_s[2 * hh + 1, :, ls]
            i0 = it_s[2 * hh, :, ls] * (4 * NKEYS)
            i1 = it_s[2 * hh + 1, :, ls] * 4
            vals, keys, pay = [], [], []
            for i, jb, jmax in ((0, 0, 8), (0, 1, 8), (1, 0, 8), (2, 0, 5), (3, 0, 4)):
                v = s0[i:i + 1, :] + s1[jb * 8:(jb + 1) * 8, :]
                kk = sub + (i * TOPK + jb * 8)
                pp = i0[i:i + 1, :] + i1[jb * 8:(jb + 1) * 8, :]
                if jmax < 8:
                    ok = sub < jmax
                    v = jnp.where(ok, v, neg_inf)
                    kk = jnp.where(ok, kk, big)
                vals.append(v); keys.append(kk); pay.append(pp)
            for j, ib, lo, hi in ((0, 0, 4, 8), (0, 1, 0, 8), (1, 0, 4, 8), (2, 0, 4, 5)):
                v = s0[ib * 8:(ib + 1) * 8, :] + s1[j:j + 1, :]
                kk = (sub + ib * 8) * TOPK + j
                pp = i0[ib * 8:(ib + 1) * 8, :] + i1[j:j + 1, :]
                if lo > 0 or hi < 8:
                    ok = (sub >= lo) & (sub < hi)
                    v = jnp.where(ok, v, neg_inf)
                    kk = jnp.where(ok, kk, big)
                vals.append(v); keys.append(kk); pay.append(pp)
            bv, _, bp = _extract_top(vals, keys, pay, TOPK, big)
            best = jnp.concatenate(bv, axis=0)
            ex = jnp.exp(best - best[0:1, :])
            g_s[hh, :, ls] = ex / jnp.sum(ex, axis=0, keepdims=True)
            r0 = pl.multiple_of(hh * TOPK, TOPK)
            idx_ref[0, pl.ds(r0, TOPK), ls] = jnp.concatenate(bp, axis=0)
        return carry

    lax.fori_loop(0, n_heads // ROUTE_HEAD_UNROLL, stage2, 0)

    gate_ref[...] = g_s[...].reshape(n_heads * TOPK, tt).T


def _route(h1, g2, wq, sk, tt):
    t, d = h1.shape
    n_hp = sk.shape[0]
    n_heads = n_hp // 2
    nsel = n_heads * TOPK
    const = lambda shape: pl.BlockSpec(shape, lambda i: (0,) * len(shape))
    return pl.pallas_call(
        _route_kernel,
        grid=(t // tt,),
        in_specs=[pl.BlockSpec((tt, d), lambda i: (i, 0)), const(g2.shape), const(wq.shape), const(sk.shape)],
        out_specs=[pl.BlockSpec((tt * d // (2 * LANES), LANES), lambda i: (i, 0)),
                   pl.BlockSpec((1, nsel, tt), lambda i: (i, 0, 0)),
                   pl.BlockSpec((tt, nsel), lambda i: (i, 0))],
        out_shape=[jax.ShapeDtypeStruct((t * d // (2 * LANES), LANES), I32),
                   jax.ShapeDtypeStruct((t // tt, nsel, tt), I32),
                   jax.ShapeDtypeStruct((t, nsel), F32)],
        scratch_shapes=[pltpu.VMEM((n_hp, tt, LANES), BF16),
                        pltpu.VMEM((n_hp, TOPK, tt), F32), pltpu.VMEM((n_hp, TOPK, tt), I32),
                        pltpu.VMEM((n_heads, TOPK, tt), F32)],
        compiler_params=pltpu.CompilerParams(dimension_semantics=("arbitrary",),
                                             vmem_limit_bytes=VMEM_LIMIT),
        name="route",
    )(h1, g2, wq, sk)


def _clamp_group(g, n_groups):
    return jnp.minimum(g, n_groups - 1)


def _idx_block_copy(idx_hbm, idx_s, sem, step, slot, blk):
    src = idx_hbm.at[pl.ds(pl.multiple_of(step * blk, blk), blk)]
    dst = idx_s.at[pl.ds(pl.multiple_of(slot * blk, blk), blk)]
    return pltpu.make_async_copy(src, dst, sem.at[slot])


def _idx_prefetch(idx_hbm, idx_s, sem, blk):
    i = pl.program_id(0)
    slot = lax.rem(i, 2)

    @pl.when(i == 0)
    def _():
        _idx_block_copy(idx_hbm, idx_s, sem, 0, 0, blk).start()

    _idx_block_copy(idx_hbm, idx_s, sem, i, slot, blk).wait()

    @pl.when(i + 1 < pl.num_programs(0))
    def _():
        _idx_block_copy(idx_hbm, idx_s, sem, i + 1, 1 - slot, blk).start()

    return slot * blk


def _gather_row(idx_s, tab_ref, k, off_t, tb, blk):
    return tab_ref[pl.ds(pl.multiple_of(idx_s.at[pl.ds(k * tb, blk + tb)][off_t], 4), 4), :]


def _peer_u_kernel(idx_hbm, x_ref, gate_ref, sum_ref, tab_ref, o_ref, pa_s, pb_s, ra_s, rb_s, idx_s, sem):
    tb, nsel = gate_ref.shape
    blk = nsel * tb
    n_groups = tb // PEER_GROUP
    off = _idx_prefetch(idx_hbm, idx_s, sem, blk)

    def gather(grp, p_s):
        for g in range(PEER_GROUP):
            t = grp * PEER_GROUP + g
            xw = x_ref[pl.ds(pl.multiple_of(t * 4, 4), 4), :]
            xb = pltpu.bitcast(xw, BF16)
            ls = slice((g % 2) * LANES, (g % 2 + 1) * LANES)
            for k in range(nsel):
                row = _gather_row(idx_s, tab_ref, k, off + t, tb, blk)
                p_s[g // 2, k * SUBLANES:(k + 1) * SUBLANES, ls] = pltpu.bitcast(row, BF16) * xb

    def chunk_sums(p_s, r_s):
        for g2 in range(PEER_GROUP // 2):
            r_s[g2] = _mm(sum_ref[...], p_s[g2])

    def epilogue(grp, r_s):
        acts = []
        for g2 in range(PEER_GROUP // 2):
            for half in range(2):
                rt = r_s[g2, :, half * LANES:(half + 1) * LANES].T
                part = rt[0:SUBLANES, :]
                for j in range(1, LANES // SUBLANES):
                    part = part + rt[j * SUBLANES:(j + 1) * SUBLANES, :]
                acts.append(jnp.sum(part, axis=0, keepdims=True))
        r0 = pl.multiple_of(grp * PEER_GROUP, PEER_GROUP)
        o_ref[pl.ds(r0, PEER_GROUP), :] = (gate_ref[pl.ds(r0, PEER_GROUP), :]
                                           * _gelu(jnp.concatenate(acts, axis=0)))

    rb_s[...] = jnp.zeros_like(rb_s)
    gather(0, pa_s)

    def body(m, carry):
        for u in range(PEER_BODY_PAIRS):
            g0 = 2 * (m * PEER_BODY_PAIRS + u)
            epilogue(jnp.maximum(g0 - 1, 0), rb_s)
            chunk_sums(pa_s, ra_s)
            gather(g0 + 1, pb_s)
            epilogue(g0, ra_s)
            chunk_sums(pb_s, rb_s)
            gather(_clamp_group(g0 + 2, n_groups), pa_s)
        return carry

    lax.fori_loop(0, n_groups // (2 * PEER_BODY_PAIRS), body, 0)
    epilogue(n_groups - 1, rb_s)


def _idx_scratch(nsel, tb):
    return [pltpu.SMEM((2 * nsel * tb,), I32), pltpu.SemaphoreType.DMA((2,))]


def _peer_u(idx, xpk, gate, summat, table, tb):
    t, nsel = gate.shape
    return pl.pallas_call(
        _peer_u_kernel,
        grid=(t // tb,),
        in_specs=[pl.BlockSpec(memory_space=pl.ANY),
                  pl.BlockSpec((tb * 4, LANES), lambda i: (i, 0)),
                  pl.BlockSpec((tb, nsel), lambda i: (i, 0)),
                  pl.BlockSpec(memory_space=pltpu.VMEM),
                  pl.BlockSpec(memory_space=pltpu.VMEM)],
        out_specs=pl.BlockSpec((tb, nsel), lambda i: (i, 0)),
        out_shape=jax.ShapeDtypeStruct((t, nsel), F32),
        scratch_shapes=[pltpu.VMEM((PEER_GROUP // 2, nsel * SUBLANES, 2 * LANES), BF16),
                        pltpu.VMEM((PEER_GROUP // 2, nsel * SUBLANES, 2 * LANES), BF16),
                        pltpu.VMEM((PEER_GROUP // 2, nsel, 2 * LANES), F32),
                        pltpu.VMEM((PEER_GROUP // 2, nsel, 2 * LANES), F32)] + _idx_scratch(nsel, tb),
        compiler_params=pltpu.CompilerParams(dimension_semantics=("arbitrary",),
                                             vmem_limit_bytes=VMEM_LIMIT),
        name="peer_u",
    )(idx, xpk, gate, summat, table)


def _peer_v_kernel(idx_hbm, w_ref, tab_ref, o_ref, wba_s, wbb_s, idx_s, sem):
    tb, nsel = w_ref.shape
    blk = nsel * tb
    n_groups = tb // PEER_GROUP
    off = _idx_prefetch(idx_hbm, idx_s, sem, blk)

    def prep(grp, wb_s):
        r0 = pl.multiple_of(grp * PEER_GROUP, PEER_GROUP)
        wr = w_ref[pl.ds(r0, PEER_GROUP), :].astype(BF16).astype(F32)
        wbits = pltpu.bitcast(wr, I32)
        wdup = wbits | lax.shift_right_logical(wbits, 16)
        for g in range(PEER_GROUP):
            wb_s[g] = jnp.broadcast_to(wdup[g:g + 1, :], (nsel, nsel)).T

    def accumulate(grp, wb_s):
        for g in range(PEER_GROUP):
            t = grp * PEER_GROUP + g
            acc = jnp.zeros((SUBLANES, LANES), F32)
            for k0 in range(0, nsel, PEER_V_BF16_RUN):
                run = None
                for k in range(k0, k0 + PEER_V_BF16_RUN):
                    row = _gather_row(idx_s, tab_ref, k, off + t, tb, blk)
                    wk = jnp.broadcast_to(wb_s[g, k:k + 1, :], (4, LANES))
                    prod = pltpu.bitcast(row, BF16) * pltpu.bitcast(wk, BF16)
                    run = prod if run is None else run + prod
                acc = acc + run.astype(F32)
            r0 = pl.multiple_of(t * SUBLANES, SUBLANES)
            o_ref[pl.ds(r0, SUBLANES), :] = acc

    prep(0, wba_s)

    def body(m, carry):
        g0 = 2 * m
        prep(g0 + 1, wbb_s)
        accumulate(g0, wba_s)
        prep(_clamp_group(g0 + 2, n_groups), wba_s)
        accumulate(g0 + 1, wbb_s)
        return carry

    lax.fori_loop(0, n_groups // 2, body, 0)


def _peer_v(idx, w, table, tb):
    t, nsel = w.shape
    return pl.pallas_call(
        _peer_v_kernel,
        grid=(t // tb,),
        in_specs=[pl.BlockSpec(memory_space=pl.ANY),
                  pl.BlockSpec((tb, nsel), lambda i: (i, 0)),
                  pl.BlockSpec(memory_space=pltpu.VMEM)],
        out_specs=pl.BlockSpec((tb * SUBLANES, LANES), lambda i: (i, 0)),
        out_shape=jax.ShapeDtypeStruct((t * SUBLANES, LANES), F32),
        scratch_shapes=[pltpu.VMEM((PEER_GROUP, nsel, nsel), I32),
                        pltpu.VMEM((PEER_GROUP, nsel, nsel), I32)] + _idx_scratch(nsel, tb),
        compiler_params=pltpu.CompilerParams(dimension_semantics=("arbitrary",),
                                             vmem_limit_bytes=VMEM_LIMIT),
        name="peer_v",
    )(idx, w, table)


def _final_kernel(h_ref, p_ref, g_ref, o_ref):
    tt, d = h_ref.shape
    n_chunks = d // LANES
    p = jnp.concatenate([p_ref[pl.ds(c, tt, stride=n_chunks), :] for c in range(n_chunks)], axis=1)
    o_ref[...] = _rms(h_ref[...] + p, g_ref[...])


def _final(h1, pout_fold, g, tt):
    t, d = h1.shape
    return pl.pallas_call(
        _final_kernel,
        grid=(t // tt,),
        in_specs=[pl.BlockSpec((tt, d), lambda i: (i, 0)),
                  pl.BlockSpec((tt * d // LANES, LANES), lambda i: (i, 0)),
                  pl.BlockSpec((1, d), lambda i: (0, 0))],
        out_specs=pl.BlockSpec((tt, d), lambda i: (i, 0)),
        out_shape=jax.ShapeDtypeStruct((t, d), F32),
        compiler_params=pltpu.CompilerParams(dimension_semantics=("arbitrary",)),
        name="final",
    )(h1, pout_fold, g)


def _pack_table_kernel(t_ref, o_ref):
    _pack_rows_store(t_ref[...].astype(BF16).astype(F32), o_ref)


def _pack_table(table, rows):
    n, d = table.shape
    return pl.pallas_call(
        _pack_table_kernel,
        grid=(n // rows,),
        in_specs=[pl.BlockSpec((rows, d), lambda i: (i, 0))],
        out_specs=pl.BlockSpec((rows * d // (2 * LANES), LANES), lambda i: (i, 0)),
        out_shape=jax.ShapeDtypeStruct((n * d // (2 * LANES), LANES), I32),
        compiler_params=pltpu.CompilerParams(dimension_semantics=("arbitrary",)),
        name="pack_table",
    )(table)


def kernel(x, norm1_g, w_in, conv_a_w, conv_a_b, w_a, b_a, w_x, b_x, lru_lambda, conv_b_w, w_out, norm2_g,
           peer_wq, peer_subkeys, peer_u, peer_v, final_g):
    b, s, d = x.shape
    depth = norm1_g.shape[0]
    t = b * s
    h = x
    for l in range(depth):
        wax = jnp.concatenate([w_a[l], w_x[l]], axis=-1).astype(BF16)
        h = _mixer(h, norm1_g[l][None], w_in[l].astype(BF16), conv_a_w[l], conv_a_b[l][None], wax,
                   b_a[l][None], b_x[l][None], lru_lambda[l][None], conv_b_w[l], w_out[l].astype(BF16),
                   min(MIX_TS, s))
        h1 = h.reshape(t, d)
        n_heads, _, nkeys, half = peer_subkeys[l].shape
        sk = peer_subkeys[l].reshape(n_heads * 2, nkeys, half).astype(BF16)
        xpk, idx, gate = _route(h1, norm2_g[l][None], peer_wq[l].astype(BF16), sk, min(ROUTE_TT, t))
        nsel = n_heads * TOPK
        assert nsel == LANES and d == SUBLANES * LANES and nkeys == NKEYS and ROUTE_TT == PEER_TB
        idx = idx.reshape(-1)
        summat = (jnp.arange(nsel * SUBLANES)[None, :] // SUBLANES == jnp.arange(nsel)[:, None]).astype(BF16)
        wts = _peer_u(idx, xpk, gate, summat, _pack_table(peer_u[l], PACK_ROWS), min(PEER_TB, t))
        pout_fold = _peer_v(idx, wts, _pack_table(peer_v[l], PACK_ROWS), min(PEER_TB, t))
        if l + 1 < depth:
            h = (h1 + pout_fold.reshape(t, d)).reshape(b, s, d)
    return _final(h1, pout_fold, final_g[None], min(ROUTE_TT, t)).reshape(b, s, d)
```

```python
import math

import jax
import jax.numpy as jnp
from jax import lax
from jax.experimental import pallas as pl
from jax.experimental.pallas import tpu as pltpu

F32 = jnp.float32
BF16 = jnp.bfloat16
I32 = jnp.int32

EPS = 1e-6
LRU_C = 8.0
LANES = 128
SUBLANES = 8
TOPK = 16
NKEYS = 128
VMEM_LIMIT = 56 * 1024 * 1024

MIX_TS = 256
ROUTE_TT = 512
ROUTE_HEAD_UNROLL = 2
ROUTE_HP_UNROLL = 4
PEER_TB = 512
PACK_ROWS = 512
PEER_GROUP = 8
PEER_V_BF16_RUN = 4


def _gelu(x):
    return 0.5 * x * (1.0 + jnp.tanh(math.sqrt(2.0 / math.pi) * (x + 0.044715 * (x * x * x))))


def _sigmoid(x):
    return 0.5 * (jnp.tanh(0.5 * x) + 1.0)


def _mm(a, b):
    return jnp.dot(a, b, preferred_element_type=F32)


def _mm_nt(a, b):
    return lax.dot_general(a, b, (((1,), (1,)), ((), ())), preferred_element_type=F32)


def _rms(x, g):
    return x * lax.rsqrt(jnp.mean(x * x, axis=-1, keepdims=True) + EPS) * g


def _mixer_kernel(x_ref, g1_ref, win_ref, caw_ref, cab_ref, wax_ref, ba_ref, bx_ref, lam_ref, cbw_ref,
                  wout_ref, o_ref, xa_ext, cv_ext, a_s, u_s, h_s, hc_s):
    ts, d = a_s.shape
    hd = wax_ref.shape[1]
    n_heads = wax_ref.shape[0]

    @pl.when(pl.program_id(1) == 0)
    def _():
        xa_ext[0:SUBLANES, :] = jnp.zeros((SUBLANES, d), F32)
        cv_ext[0:SUBLANES, :] = jnp.zeros((SUBLANES, d), F32)
        hc_s[...] = jnp.zeros_like(hc_s)

    x = x_ref[0]
    xn = _rms(x, g1_ref[...]).astype(BF16)

    def proj(j):
        return _mm(xn, win_ref[:, j * d:(j + 1) * d])

    xa_ext[SUBLANES:SUBLANES + ts, :] = proj(0)
    xc = cab_ref[...]
    for k in range(4):
        xc = xc + caw_ref[k:k + 1, :] * xa_ext[SUBLANES - 3 + k:SUBLANES - 3 + k + ts, :]
    xa_ext[0:SUBLANES, :] = xa_ext[ts:ts + SUBLANES, :]

    lam = lam_ref[...]
    neg_c_sp = -LRU_C * (jnp.maximum(-lam, 0.0) + jnp.log1p(jnp.exp(-jnp.abs(lam))))
    xcb = xc.astype(BF16)
    for h in range(n_heads):
        sl = slice(h * hd, (h + 1) * hd)
        gates = _mm(xcb[:, sl], wax_ref[h])
        r = _sigmoid(gates[:, :hd] + ba_ref[:, sl])
        i = _sigmoid(gates[:, hd:] + bx_ref[:, sl])
        log_a = r * neg_c_sp[:, sl]
        a = jnp.exp(log_a)
        mult = jnp.sqrt(1.0 - a * a)
        a_s[:, sl] = a
        u_s[:, sl] = mult * (i * xc[:, sl])

    row = lax.broadcasted_iota(I32, (SUBLANES, d), 0)

    def scan_body(g, hprev):
        r0 = g * SUBLANES
        av = a_s[pl.ds(r0, SUBLANES), :]
        uv = u_s[pl.ds(r0, SUBLANES), :]
        for sh in (1, 2, 4):
            keep = row >= sh
            a_sh = jnp.where(keep, pltpu.roll(av, sh, axis=0), 1.0)
            u_sh = jnp.where(keep, pltpu.roll(uv, sh, axis=0), 0.0)
            uv = uv + av * u_sh
            av = av * a_sh
        hv = uv + av * hprev
        h_s[pl.ds(r0, SUBLANES), :] = hv
        return hv[SUBLANES - 1:SUBLANES, :]

    hprev = hc_s[...]
    for g in range(ts // SUBLANES):
        hprev = scan_body(g, hprev)
    hc_s[...] = hprev

    ya = h_s[...] * _gelu(proj(1))

    cv_ext[SUBLANES:SUBLANES + ts, :] = proj(4) * proj(2)
    conv = cbw_ref[0:1, :] * cv_ext[SUBLANES - 2:SUBLANES - 2 + ts, :]
    for k in range(1, 3):
        conv = conv + cbw_ref[k:k + 1, :] * cv_ext[SUBLANES - 2 + k:SUBLANES - 2 + k + ts, :]
    cv_ext[0:SUBLANES, :] = cv_ext[ts:ts + SUBLANES, :]
    yb = proj(3) * conv

    merged = _sigmoid(proj(5)) * ya + _sigmoid(proj(6)) * yb
    o_ref[0] = x + _mm(merged.astype(BF16), wout_ref[...])


def _mixer(x, g1, w_in, caw, cab, wax, ba, bx, lam, cbw, w_out, ts):
    b, s, d = x.shape
    const = lambda shape: pl.BlockSpec(shape, lambda i, j: (0,) * len(shape))
    return pl.pallas_call(
        _mixer_kernel,
        grid=(b, s // ts),
        in_specs=[pl.BlockSpec((1, ts, d), lambda i, j: (i, j, 0)),
                  const(g1.shape), const(w_in.shape), const(caw.shape), const(cab.shape), const(wax.shape),
                  const(ba.shape), const(bx.shape), const(lam.shape), const(cbw.shape), const(w_out.shape)],
        out_specs=pl.BlockSpec((1, ts, d), lambda i, j: (i, j, 0)),
        out_shape=jax.ShapeDtypeStruct(x.shape, F32),
        scratch_shapes=[pltpu.VMEM((ts + SUBLANES, d), F32), pltpu.VMEM((ts + SUBLANES, d), F32),
                        pltpu.VMEM((ts, d), F32), pltpu.VMEM((ts, d), F32), pltpu.VMEM((ts, d), F32),
                        pltpu.VMEM((1, d), F32)],
        compiler_params=pltpu.CompilerParams(dimension_semantics=("arbitrary", "arbitrary"),
                                             vmem_limit_bytes=VMEM_LIMIT),
        name="mixer",
    )(x, g1, w_in, caw, cab, wax, ba, bx, lam, cbw, w_out)


def _extract_top(vals, keys, payload, n, key_sentinel):
    out_v, out_k, out_p = [], [], []
    neg_inf = jnp.float32(-jnp.inf)
    for _ in range(n):
        m = vals[0]
        for v in vals[1:]:
            m = jnp.maximum(m, v)
        m = jnp.max(m, axis=0, keepdims=True)
        cand = [jnp.where(v == m, k, key_sentinel) for v, k in zip(vals, keys)]
        am = cand[0]
        for c in cand[1:]:
            am = jnp.minimum(am, c)
        am = jnp.min(am, axis=0, keepdims=True)
        hit = [k == am for k in keys]
        out_v.append(m)
        out_k.append(am)
        if payload is not None:
            p = jnp.where(hit[0], payload[0], -1)
            for hh, pp in zip(hit[1:], payload[1:]):
                p = jnp.maximum(p, jnp.where(hh, pp, -1))
            out_p.append(jnp.max(p, axis=0, keepdims=True))
        vals = [jnp.where(hh, neg_inf, v) for hh, v in zip(hit, vals)]
    return out_v, out_k, out_p


def _pack_rows_store(x, out_ref):
    n, d = x.shape
    n_words = d // (2 * LANES)
    bits = pltpu.bitcast(x, I32)
    for s in range(n_words):
        lo = bits[:, (2 * s) * LANES:(2 * s + 1) * LANES]
        hi = bits[:, (2 * s + 1) * LANES:(2 * s + 2) * LANES]
        out_ref[pl.ds(s, n, stride=n_words), :] = lax.shift_right_logical(lo, 16) | (hi & jnp.int32(-65536))


def _route_kernel(h_ref, g2_ref, wq_ref, sk_ref, xpk_ref, idx_ref, gate_ref, q_s, st_s, it_s, g_s):
    tt, d = h_ref.shape
    n_hp = sk_ref.shape[0]
    n_heads = n_hp // 2
    n_lt = tt // LANES

    xn = _rms(h_ref[...], g2_ref[...]).astype(BF16)
    _pack_rows_store(xn.astype(F32), xpk_ref)
    q = _mm(xn, wq_ref[...]).astype(BF16)
    for hp in range(n_hp):
        q_s[hp] = q[:, hp * LANES:(hp + 1) * LANES]

    sub = lax.broadcasted_iota(I32, (SUBLANES, LANES), 0)

    def stage1(it, carry):
        for u in range(ROUTE_HP_UNROLL):
            hp = it * ROUTE_HP_UNROLL + u
            sc = _mm_nt(sk_ref[hp], q_s[hp])
            for lt in range(n_lt):
                blk = [sc[j * SUBLANES:(j + 1) * SUBLANES, lt * LANES:(lt + 1) * LANES]
                       for j in range(NKEYS // SUBLANES)]
                keys = [sub + j * SUBLANES for j in range(NKEYS // SUBLANES)]
                v, k, _ = _extract_top(blk, keys, None, TOPK, NKEYS)
                st_s[hp, :, lt * LANES:(lt + 1) * LANES] = jnp.concatenate(v, axis=0)
                it_s[hp, :, lt * LANES:(lt + 1) * LANES] = jnp.concatenate(k, axis=0)
        return carry

    lax.fori_loop(0, n_hp // ROUTE_HP_UNROLL, stage1, 0)

    big = TOPK * TOPK
    neg_inf = jnp.float32(-jnp.inf)

    def stage2(it, carry):
        for hh, lt in [(it * ROUTE_HEAD_UNROLL + u, lt) for u in range(ROUTE_HEAD_UNROLL) for lt in range(n_lt)]:
            ls = slice(lt * LANES, (lt + 1) * LANES)
            s0 = st_s[2 * hh, :, ls]
            s1 = st_s[2 * hh + 1, :, ls]
            i0 = it_s[2 * hh, :, ls] * (4 * NKEYS)
            i1 = it_s[2 * hh + 1, :, ls] * 4
            vals, keys, pay = [], [], []
            for i, jb, jmax in ((0, 0, 8), (0, 1, 8), (1, 0, 8), (2, 0, 5), (3, 0, 4)):
                v = s0[i:i + 1, :] + s1[jb * 8:(jb + 1) * 8, :]
                kk = sub + (i * TOPK + jb * 8)
                pp = i0[i:i + 1, :] + i1[jb * 8:(jb + 1) * 8, :]
                if jmax < 8:
                    ok = sub < jmax
                    v = jnp.where(ok, v, neg_inf)
                    kk = jnp.where(ok, kk, big)
                vals.append(v); keys.append(kk); pay.append(pp)
            for j, ib, lo, hi in ((0, 0, 4, 8), (0, 1, 0, 8), (1, 0, 4, 8), (2, 0, 4, 5)):
                v = s0[ib * 8:(ib + 1) * 8, :] + s1[j:j + 1, :]
                kk = (sub + ib * 8) * TOPK + j
                pp = i0[ib * 8:(ib + 1) * 8, :] + i1[j:j + 1, :]
                if lo > 0 or hi < 8:
                    ok = (sub >= lo) & (sub < hi)
                    v = jnp.where(ok, v, neg_inf)
                    kk = jnp.where(ok, kk, big)
                vals.append(v); keys.append(kk); pay.append(pp)
            bv, _, bp = _extract_top(vals, keys, pay, TOPK, big)
            best = jnp.concatenate(bv, axis=0)
            ex = jnp.exp(best - best[0:1, :])
            g_s[hh, :, ls] = ex / jnp.sum(ex, axis=0, keepdims=True)
            r0 = pl.multiple_of(hh * TOPK, TOPK)
            idx_ref[0, pl.ds(r0, TOPK), ls] = jnp.concatenate(bp, axis=0)
        return carry

    lax.fori_loop(0, n_heads // ROUTE_HEAD_UNROLL, stage2, 0)

    gate_ref[...] = g_s[...].reshape(n_heads * TOPK, tt).T


def _route(h1, g2, wq, sk, tt):
    t, d = h1.shape
    n_hp = sk.shape[0]
    n_heads = n_hp // 2
    nsel = n_heads * TOPK
    const = lambda shape: pl.BlockSpec(shape, lambda i: (0,) * len(shape))
    return pl.pallas_call(
        _route_kernel,
        grid=(t // tt,),
        in_specs=[pl.BlockSpec((tt, d), lambda i: (i, 0)), const(g2.shape), const(wq.shape), const(sk.shape)],
        out_specs=[pl.BlockSpec((tt * d // (2 * LANES), LANES), lambda i: (i, 0)),
                   pl.BlockSpec((1, nsel, tt), lambda i: (i, 0, 0)),
                   pl.BlockSpec((tt, nsel), lambda i: (i, 0))],
        out_shape=[jax.ShapeDtypeStruct((t * d // (2 * LANES), LANES), I32),
                   jax.ShapeDtypeStruct((t // tt, nsel, tt), I32),
                   jax.ShapeDtypeStruct((t, nsel), F32)],
        scratch_shapes=[pltpu.VMEM((n_hp, tt, LANES), BF16),
                        pltpu.VMEM((n_hp, TOPK, tt), F32), pltpu.VMEM((n_hp, TOPK, tt), I32),
                        pltpu.VMEM((n_heads, TOPK, tt), F32)],
        compiler_params=pltpu.CompilerParams(dimension_semantics=("arbitrary",),
                                             vmem_limit_bytes=VMEM_LIMIT),
        name="route",
    )(h1, g2, wq, sk)


def _clamp_group(g, n_groups):
    return jnp.minimum(g, n_groups - 1)


def _idx_block_copy(idx_hbm, idx_s, sem, step, slot, blk):
    src = idx_hbm.at[pl.ds(pl.multiple_of(step * blk, blk), blk)]
    dst = idx_s.at[pl.ds(pl.multiple_of(slot * blk, blk), blk)]
    return pltpu.make_async_copy(src, dst, sem.at[slot])


def _idx_prefetch(idx_hbm, idx_s, sem, blk):
    i = pl.program_id(0)
    slot = lax.rem(i, 2)

    @pl.when(i == 0)
    def _():
        _idx_block_copy(idx_hbm, idx_s, sem, 0, 0, blk).start()

    _idx_block_copy(idx_hbm, idx_s, sem, i, slot, blk).wait()

    @pl.when(i + 1 < pl.num_programs(0))
    def _():
        _idx_block_copy(idx_hbm, idx_s, sem, i + 1, 1 - slot, blk).start()

    return slot * blk


def _gather_row(idx_s, tab_ref, k, off_t, tb, blk):
    return tab_ref[pl.ds(pl.multiple_of(idx_s.at[pl.ds(k * tb, blk + tb)][off_t], 4), 4), :]


def _peer_u_kernel(idx_hbm, x_ref, gate_ref, sum_ref, tab_ref, o_ref, pa_s, pb_s, ra_s, rb_s, idx_s, sem):
    tb, nsel = gate_ref.shape
    blk = nsel * tb
    n_groups = tb // PEER_GROUP
    off = _idx_prefetch(idx_hbm, idx_s, sem, blk)

    def gather(grp, p_s):
        for g in range(PEER_GROUP):
            t = grp * PEER_GROUP + g
            xw = x_ref[pl.ds(pl.multiple_of(t * 4, 4), 4), :]
            xb = pltpu.bitcast(xw, BF16)
            for k in range(nsel):
                row = _gather_row(idx_s, tab_ref, k, off + t, tb, blk)
                p_s[g, k * SUBLANES:(k + 1) * SUBLANES, :] = pltpu.bitcast(row, BF16) * xb

    def chunk_sums(p_s, r_s):
        for g2 in range(PEER_GROUP // 2):
            pair = jnp.concatenate([p_s[2 * g2], p_s[2 * g2 + 1]], axis=1)
            r_s[g2] = _mm(sum_ref[...], pair)

    def epilogue(grp, r_s):
        acts = []
        for g2 in range(PEER_GROUP // 2):
            for half in range(2):
                rt = r_s[g2, :, half * LANES:(half + 1) * LANES].T
                part = rt[0:SUBLANES, :]
                for j in range(1, LANES // SUBLANES):
                    part = part + rt[j * SUBLANES:(j + 1) * SUBLANES, :]
                acts.append(jnp.sum(part, axis=0, keepdims=True))
        r0 = pl.multiple_of(grp * PEER_GROUP, PEER_GROUP)
        o_ref[pl.ds(r0, PEER_GROUP), :] = (gate_ref[pl.ds(r0, PEER_GROUP), :]
                                           * _gelu(jnp.concatenate(acts, axis=0)))

    rb_s[...] = jnp.zeros_like(rb_s)
    gather(0, pa_s)

    def body(m, carry):
        g0 = 2 * m
        epilogue(jnp.maximum(g0 - 1, 0), rb_s)
        chunk_sums(pa_s, ra_s)
        gather(g0 + 1, pb_s)
        epilogue(g0, ra_s)
        chunk_sums(pb_s, rb_s)
        gather(_clamp_group(g0 + 2, n_groups), pa_s)
        return carry

    lax.fori_loop(0, n_groups // 2, body, 0)
    epilogue(n_groups - 1, rb_s)


def _idx_scratch(nsel, tb):
    return [pltpu.SMEM((2 * nsel * tb,), I32), pltpu.SemaphoreType.DMA((2,))]


def _peer_u(idx, xpk, gate, summat, table, tb):
    t, nsel = gate.shape
    return pl.pallas_call(
        _peer_u_kernel,
        grid=(t // tb,),
        in_specs=[pl.BlockSpec(memory_space=pl.ANY),
                  pl.BlockSpec((tb * 4, LANES), lambda i: (i, 0)),
                  pl.BlockSpec((tb, nsel), lambda i: (i, 0)),
                  pl.BlockSpec(memory_space=pltpu.VMEM),
                  pl.BlockSpec(memory_space=pltpu.VMEM)],
        out_specs=pl.BlockSpec((tb, nsel), lambda i: (i, 0)),
        out_shape=jax.ShapeDtypeStruct((t, nsel), F32),
        scratch_shapes=[pltpu.VMEM((PEER_GROUP, nsel * SUBLANES, LANES), BF16),
                        pltpu.VMEM((PEER_GROUP, nsel * SUBLANES, LANES), BF16),
                        pltpu.VMEM((PEER_GROUP // 2, nsel, 2 * LANES), F32),
                        pltpu.VMEM((PEER_GROUP // 2, nsel, 2 * LANES), F32)] + _idx_scratch(nsel, tb),
        compiler_params=pltpu.CompilerParams(dimension_semantics=("arbitrary",),
                                             vmem_limit_bytes=VMEM_LIMIT),
        name="peer_u",
    )(idx, xpk, gate, summat, table)


def _peer_v_kernel(idx_hbm, w_ref, tab_ref, o_ref, wba_s, wbb_s, idx_s, sem):
    tb, nsel = w_ref.shape
    blk = nsel * tb
    n_groups = tb // PEER_GROUP
    off = _idx_prefetch(idx_hbm, idx_s, sem, blk)

    def prep(grp, wb_s):
        r0 = pl.multiple_of(grp * PEER_GROUP, PEER_GROUP)
        wr = w_ref[pl.ds(r0, PEER_GROUP), :].astype(BF16).astype(F32)
        wbits = pltpu.bitcast(wr, I32)
        wdup = wbits | lax.shift_right_logical(wbits, 16)
        for g in range(PEER_GROUP):
            wb_s[g] = jnp.broadcast_to(wdup[g:g + 1, :], (nsel, nsel)).T

    def accumulate(grp, wb_s):
        for g in range(PEER_GROUP):
            t = grp * PEER_GROUP + g
            acc = jnp.zeros((SUBLANES, LANES), F32)
            for k0 in range(0, nsel, PEER_V_BF16_RUN):
                run = None
                for k in range(k0, k0 + PEER_V_BF16_RUN):
                    row = _gather_row(idx_s, tab_ref, k, off + t, tb, blk)
                    wk = jnp.broadcast_to(wb_s[g, k:k + 1, :], (4, LANES))
                    prod = pltpu.bitcast(row, BF16) * pltpu.bitcast(wk, BF16)
                    run = prod if run is None else run + prod
                acc = acc + run.astype(F32)
            r0 = pl.multiple_of(t * SUBLANES, SUBLANES)
            o_ref[pl.ds(r0, SUBLANES), :] = acc

    prep(0, wba_s)

    def body(m, carry):
        g0 = 2 * m
        prep(g0 + 1, wbb_s)
        accumulate(g0, wba_s)
        prep(_clamp_group(g0 + 2, n_groups), wba_s)
        accumulate(g0 + 1, wbb_s)
        return carry

    lax.fori_loop(0, n_groups // 2, body, 0)


def _peer_v(idx, w, table, tb):
    t, nsel = w.shape
    return pl.pallas_call(
        _peer_v_kernel,
        grid=(t // tb,),
        in_specs=[pl.BlockSpec(memory_space=pl.ANY),
                  pl.BlockSpec((tb, nsel), lambda i: (i, 0)),
                  pl.BlockSpec(memory_space=pltpu.VMEM)],
        out_specs=pl.BlockSpec((tb * SUBLANES, LANES), lambda i: (i, 0)),
        out_shape=jax.ShapeDtypeStruct((t * SUBLANES, LANES), F32),
        scratch_shapes=[pltpu.VMEM((PEER_GROUP, nsel, nsel), I32),
                        pltpu.VMEM((PEER_GROUP, nsel, nsel), I32)] + _idx_scratch(nsel, tb),
        compiler_params=pltpu.CompilerParams(dimension_semantics=("arbitrary",),
                                             vmem_limit_bytes=VMEM_LIMIT),
        name="peer_v",
    )(idx, w, table)


def _final_kernel(h_ref, p_ref, g_ref, o_ref):
    tt, d = h_ref.shape
    n_chunks = d // LANES
    p = jnp.concatenate([p_ref[pl.ds(c, tt, stride=n_chunks), :] for c in range(n_chunks)], axis=1)
    o_ref[...] = _rms(h_ref[...] + p, g_ref[...])


def _final(h1, pout_fold, g, tt):
    t, d = h1.shape
    return pl.pallas_call(
        _final_kernel,
        grid=(t // tt,),
        in_specs=[pl.BlockSpec((tt, d), lambda i: (i, 0)),
                  pl.BlockSpec((tt * d // LANES, LANES), lambda i: (i, 0)),
                  pl.BlockSpec((1, d), lambda i: (0, 0))],
        out_specs=pl.BlockSpec((tt, d), lambda i: (i, 0)),
        out_shape=jax.ShapeDtypeStruct((t, d), F32),
        compiler_params=pltpu.CompilerParams(dimension_semantics=("arbitrary",)),
        name="final",
    )(h1, pout_fold, g)


def _pack_table_kernel(t_ref, o_ref):
    _pack_rows_store(t_ref[...].astype(BF16).astype(F32), o_ref)


def _pack_table(table, rows):
    n, d = table.shape
    return pl.pallas_call(
        _pack_table_kernel,
        grid=(n // rows,),
        in_specs=[pl.BlockSpec((rows, d), lambda i: (i, 0))],
        out_specs=pl.BlockSpec((rows * d // (2 * LANES), LANES), lambda i: (i, 0)),
        out_shape=jax.ShapeDtypeStruct((n * d // (2 * LANES), LANES), I32),
        compiler_params=pltpu.CompilerParams(dimension_semantics=("arbitrary",)),
        name="pack_table",
    )(table)


def kernel(x, norm1_g, w_in, conv_a_w, conv_a_b, w_a, b_a, w_x, b_x, lru_lambda, conv_b_w, w_out, norm2_g,
           peer_wq, peer_subkeys, peer_u, peer_v, final_g):
    b, s, d = x.shape
    depth = norm1_g.shape[0]
    t = b * s
    h = x
    for l in range(depth):
        wax = jnp.concatenate([w_a[l], w_x[l]], axis=-1).astype(BF16)
        h = _mixer(h, norm1_g[l][None], w_in[l].astype(BF16), conv_a_w[l], conv_a_b[l][None], wax,
                   b_a[l][None], b_x[l][None], lru_lambda[l][None], conv_b_w[l], w_out[l].astype(BF16),
                   min(MIX_TS, s))
        h1 = h.reshape(t, d)
        n_heads, _, nkeys, half = peer_subkeys[l].shape
        sk = peer_subkeys[l].reshape(n_heads * 2, nkeys, half).astype(BF16)
        xpk, idx, gate = _route(h1, norm2_g[l][None], peer_wq[l].astype(BF16), sk, min(ROUTE_TT, t))
        nsel = n_heads * TOPK
        assert nsel == LANES and d == SUBLANES * LANES and nkeys == NKEYS and ROUTE_TT == PEER_TB
        idx = idx.reshape(-1)
        summat = (jnp.arange(nsel * SUBLANES)[None, :] // SUBLANES == jnp.arange(nsel)[:, None]).astype(BF16)
        wts = _peer_u(idx, xpk, gate, summat, _pack_table(peer_u[l], PACK_ROWS), min(PEER_TB, t))
        pout_fold = _peer_v(idx, wts, _pack_table(peer_v[l], PACK_ROWS), min(PEER_TB, t))
        if l + 1 < depth:
            h = (h1 + pout_fold.reshape(t, d)).reshape(b, s, d)
    return _final(h1, pout_fold, final_g[None], min(ROUTE_TT, t)).reshape(b, s, d)
```
